```python
import math
import jax, jax.numpy as jnp
from jax import lax
import numpy as np

D_MODEL = 1024
BATCH = 4
SEQ = 8192
DEPTH = 2

N_EVEN = (DEPTH + 1) // 2
N_ODD = DEPTH // 2
LRU_WIDTH = D_MODEL
LRU_BLOCKS = 8
LRU_BLOCK = LRU_WIDTH // LRU_BLOCKS
LRU_C = 8.0
CONV_WIDTH = 4
SB_HEADS = 8
SB_HEAD_DIM = 128
SB_WIDTH = SB_HEADS * SB_HEAD_DIM
C_HEADS = 16
C_KV_HEADS = 2
C_GROUP = C_HEADS // C_KV_HEADS
C_HEAD_DIM = 64
C_WIDTH = C_HEADS * C_HEAD_DIM
C_KV_WIDTH = C_KV_HEADS * C_HEAD_DIM
WINDOW = 128
Q_BLOCK = 128
EVEN_IN = 2 * LRU_WIDTH + 4 * SB_WIDTH
EVEN_MIX = LRU_WIDTH + SB_WIDTH
ODD_IN = 2 * C_WIDTH + 2 * C_KV_WIDTH
DEEPNORM_ALPHA = float((2 * DEPTH) ** 0.25)
DEEPNORM_BETA = float((8 * DEPTH) ** -0.25)
LN_EPS = 1e-5

kernel_name = "hybrid_rglru_stickbreak_swa_deepnorm"


def _layer_norm(x, g, b):
    xf = x.astype(jnp.float32)
    mu = jnp.mean(xf, axis=-1, keepdims=True)
    var = jnp.mean(jnp.square(xf - mu), axis=-1, keepdims=True)
    y = (xf - mu) * lax.rsqrt(var + LN_EPS) * g.astype(jnp.float32) + b.astype(jnp.float32)
    return y.astype(x.dtype)


def _causal_dwconv(x, w, b):
    s = x.shape[1]
    xp = jnp.pad(x, ((0, 0), (CONV_WIDTH - 1, 0), (0, 0)))
    y = b
    for k in range(CONV_WIDTH):
        y = y + w[k] * xp[:, k:k + s]
    return y


def _rg_lru(x, w_a, b_a, w_x, b_x, lam):
    bsz, s, _ = x.shape
    xb = x.reshape(bsz, s, LRU_BLOCKS, LRU_BLOCK)
    r = jax.nn.sigmoid(jnp.einsum('bsnd,nde->bsne', xb, w_a).reshape(bsz, s, LRU_WIDTH) + b_a)
    i = jax.nn.sigmoid(jnp.einsum('bsnd,nde->bsne', xb, w_x).reshape(bsz, s, LRU_WIDTH) + b_x)
    log_a = LRU_C * r.astype(jnp.float32) * jax.nn.log_sigmoid(lam.astype(jnp.float32))
    a = jnp.exp(log_a)
    u = jnp.sqrt(-jnp.expm1(2.0 * log_a)) * (i * x).astype(jnp.float32)

    def combine(lhs, rhs):
        a1, b1 = lhs
        a2, b2 = rhs
        return a1 * a2, a2 * b1 + b2

    _, h = lax.associative_scan(combine, (a, u), axis=1)
    return h.astype(x.dtype)


def _stick_breaking(q, k, v):
    bsz, s, h, dh = q.shape
    nb = s // Q_BLOCK
    scale = 1.0 / math.sqrt(dh)
    qb = q.reshape(bsz, nb, Q_BLOCK, h, dh).transpose(1, 0, 2, 3, 4)
    kpos = jnp.arange(s)

    def one_block(args):
        n, qblk = args
        z = jnp.einsum('bqhd,bkhd->bhqk', qblk, k).astype(jnp.float32) * scale
        qpos = n * Q_BLOCK + jnp.arange(Q_BLOCK)
        causal = kpos[None, :] < qpos[:, None]
        log_1mb = jnp.where(causal, jax.nn.log_sigmoid(-z), 0.0)
        suffix = lax.cumsum(log_1mb, axis=3, reverse=True) - log_1mb
        w = jnp.where(causal, jnp.exp(jax.nn.log_sigmoid(z) + suffix), 0.0)
        return jnp.einsum('bhqk,bkhd->bqhd', w.astype(v.dtype), v)

    o = lax.map(one_block, (jnp.arange(nb), qb))
    return o.transpose(1, 0, 2, 3, 4).reshape(bsz, s, h, dh)


def _alibi_slopes(n_heads):
    return np.array([2.0 ** (-8.0 * (i + 1) / n_heads) for i in range(n_heads)], dtype=np.float32)


def _swa_sinks_alibi(q, k, v, sinks):
    bsz, s, _, dh = q.shape
    nb = s // Q_BLOCK
    scale = 1.0 / math.sqrt(dh)
    qb = q.reshape(bsz, nb, Q_BLOCK, C_KV_HEADS, C_GROUP, dh)
    kp = jnp.pad(k, ((0, 0), (Q_BLOCK, 0), (0, 0), (0, 0))).reshape(bsz, nb + 1, Q_BLOCK, C_KV_HEADS, dh)
    vp = jnp.pad(v, ((0, 0), (Q_BLOCK, 0), (0, 0), (0, 0))).reshape(bsz, nb + 1, Q_BLOCK, C_KV_HEADS, dh)
    kc = jnp.concatenate([kp[:, :-1], kp[:, 1:]], axis=2)
    vc = jnp.concatenate([vp[:, :-1], vp[:, 1:]], axis=2)
    sc = jnp.einsum('bnqcgd,bnkcd->bncgqk', qb, kc).astype(jnp.float32) * scale
    i = jnp.arange(Q_BLOCK)[:, None]
    j = jnp.arange(2 * Q_BLOCK)[None, :]
    dist = (i - j + Q_BLOCK).astype(jnp.float32)
    kpos = jnp.arange(nb)[:, None, None] * Q_BLOCK - Q_BLOCK + j[None]
    valid = (dist >= 0) & (dist < WINDOW) & (kpos >= 0)
    slopes = jnp.asarray(_alibi_slopes(C_HEADS)).reshape(C_KV_HEADS, C_GROUP)
    sc = sc - slopes[:, :, None, None] * dist
    sc = jnp.where(valid[None, :, None, None], sc, -jnp.inf)
    sink = sinks.astype(jnp.float32).reshape(C_KV_HEADS, C_GROUP)[:, :, None, None]
    m = jnp.maximum(jnp.max(sc, axis=-1, keepdims=True), sink)
    p = jnp.exp(sc - m)
    denom = jnp.sum(p, axis=-1, keepdims=True) + jnp.exp(sink - m)
    p = (p / denom).astype(v.dtype)
    o = jnp.einsum('bncgqk,bnkcd->bnqcgd', p, vc)
    return o.reshape(bsz, s, C_HEADS, dh)


def _even_layer(x, w_in, conv_w, conv_b, w_gate_a, b_gate_a, w_gate_x, b_gate_x, lru_lambda, w_out):
    bsz, s, _ = x.shape
    hproj = jnp.einsum('bsd,de->bse', x, w_in)
    splits = np.cumsum([LRU_WIDTH, LRU_WIDTH, SB_WIDTH, SB_WIDTH, SB_WIDTH])
    a_x, a_g, b_q, b_k, b_v, b_g = jnp.split(hproj, splits, axis=-1)
    a_h = _rg_lru(_causal_dwconv(a_x, conv_w, conv_b), w_gate_a, b_gate_a, w_gate_x, b_gate_x, lru_lambda)
    y_a = a_h * jax.nn.silu(a_g)
    shp = (bsz, s, SB_HEADS, SB_HEAD_DIM)
    o_b = _stick_breaking(b_q.reshape(shp), b_k.reshape(shp), b_v.reshape(shp)).reshape(bsz, s, SB_WIDTH)
    y_b = o_b * jax.nn.silu(b_g)
    y = jnp.concatenate([y_a, y_b], axis=-1)
    return jnp.einsum('bse,ed->bsd', y, w_out)


def _odd_layer(x, w_in, sinks, w_out):
    bsz, s, _ = x.shape
    hproj = jnp.einsum('bsd,de->bse', x, w_in)
    splits = np.cumsum([C_WIDTH, C_KV_WIDTH, C_KV_WIDTH])
    q, k, v, g = jnp.split(hproj, splits, axis=-1)
    o = _swa_sinks_alibi(q.reshape(bsz, s, C_HEADS, C_HEAD_DIM),
                         k.reshape(bsz, s, C_KV_HEADS, C_HEAD_DIM),
                         v.reshape(bsz, s, C_KV_HEADS, C_HEAD_DIM), sinks).reshape(bsz, s, C_WIDTH)
    y = o * jax.nn.silu(g)
    return jnp.einsum('bse,ed->bsd', y, w_out)


def setup_inputs(seed: int = 0) -> dict:
    key = jax.random.key(seed)
    ks = jax.random.split(key, 20)
    f32 = jnp.float32
    nrm = lambda k, shp, sc: jax.random.normal(k, shp, f32) * sc
    x = jax.random.normal(ks[0], (BATCH, SEQ, D_MODEL), f32)
    e_w_in = nrm(ks[1], (N_EVEN, D_MODEL, EVEN_IN), D_MODEL ** -0.5)
    e_conv_w = nrm(ks[2], (N_EVEN, CONV_WIDTH, LRU_WIDTH), CONV_WIDTH ** -0.5)
    e_conv_b = nrm(ks[3], (N_EVEN, LRU_WIDTH), 0.02)
    e_w_gate_a = nrm(ks[4], (N_EVEN, LRU_BLOCKS, LRU_BLOCK, LRU_BLOCK), LRU_BLOCK ** -0.5)
    e_b_gate_a = nrm(ks[5], (N_EVEN, LRU_WIDTH), 0.02)
    e_w_gate_x = nrm(ks[6], (N_EVEN, LRU_BLOCKS, LRU_BLOCK, LRU_BLOCK), LRU_BLOCK ** -0.5)
    e_b_gate_x = nrm(ks[7], (N_EVEN, LRU_WIDTH), 0.02)
    u = jax.random.uniform(ks[8], (N_EVEN, LRU_WIDTH), f32, 0.9, 0.999)
    a0 = u ** (1.0 / LRU_C)
    e_lru_lambda = jnp.log(a0) - jnp.log1p(-a0)
    e_w_out = nrm(ks[9], (N_EVEN, EVEN_MIX, D_MODEL), EVEN_MIX ** -0.5 * DEEPNORM_BETA)
    e_ln_g = 1.0 + nrm(ks[10], (N_EVEN, D_MODEL), 0.02)
    e_ln_b = nrm(ks[11], (N_EVEN, D_MODEL), 0.02)
    o_w_in = nrm(ks[12], (N_ODD, D_MODEL, ODD_IN), D_MODEL ** -0.5)
    o_sinks = nrm(ks[13], (N_ODD, C_HEADS), 1.0)
    o_w_out = nrm(ks[14], (N_ODD, C_WIDTH, D_MODEL), C_WIDTH ** -0.5 * DEEPNORM_BETA)
    o_ln_g = 1.0 + nrm(ks[15], (N_ODD, D_MODEL), 0.02)
    o_ln_b = nrm(ks[16], (N_ODD, D_MODEL), 0.02)
    return {"x": x, "e_w_in": e_w_in, "e_conv_w": e_conv_w, "e_conv_b": e_conv_b,
            "e_w_gate_a": e_w_gate_a, "e_b_gate_a": e_b_gate_a, "e_w_gate_x": e_w_gate_x,
            "e_b_gate_x": e_b_gate_x, "e_lru_lambda": e_lru_lambda, "e_w_out": e_w_out,
            "e_ln_g": e_ln_g, "e_ln_b": e_ln_b, "o_w_in": o_w_in, "o_sinks": o_sinks,
            "o_w_out": o_w_out, "o_ln_g": o_ln_g, "o_ln_b": o_ln_b}


def reference(x, e_w_in, e_conv_w, e_conv_b, e_w_gate_a, e_b_gate_a, e_w_gate_x, e_b_gate_x,
              e_lru_lambda, e_w_out, e_ln_g, e_ln_b, o_w_in, o_sinks, o_w_out, o_ln_g, o_ln_b):
    for layer in range(DEPTH):
        li = layer // 2
        if layer % 2 == 0:
            y = _even_layer(x, e_w_in[li], e_conv_w[li], e_conv_b[li], e_w_gate_a[li], e_b_gate_a[li],
                            e_w_gate_x[li], e_b_gate_x[li], e_lru_lambda[li], e_w_out[li])
            x = _layer_norm(DEEPNORM_ALPHA * x + y, e_ln_g[li], e_ln_b[li])
        else:
            y = _odd_layer(x, o_w_in[li], o_sinks[li], o_w_out[li])
            x = _layer_norm(DEEPNORM_ALPHA * x + y, o_ln_g[li], o_ln_b[li])
    return x
```

```python
import functools
import math

import jax
import jax.numpy as jnp
from jax import lax
from jax.experimental import pallas as pl
from jax.experimental.pallas import tpu as pltpu

F32 = jnp.float32
BF16 = jnp.bfloat16

LANES = 128
SUBLANES = 8
VMEM_BYTES_V7X = 64 * 1024 * 1024

LRU_BLOCK = 128
LRU_C = 8.0
CONV_WIDTH = 4
SB_HEAD_DIM = 128
C_HEADS = 16
C_KV_HEADS = 2
C_GROUP = C_HEADS // C_KV_HEADS
C_HEAD_DIM = 64
WINDOW = 128
LN_EPS = 1e-5


def _sigmoid(x):
    return 0.5 * jnp.tanh(0.5 * x) + 0.5


def _silu(x):
    return x * _sigmoid(x)


def _vmem_limit(nbytes):
    return int(min(nbytes, VMEM_BYTES_V7X - 8 * 1024 * 1024))


def _inproj_kernel(x_ref, w_ref, *out_refs, widths, scales):
    xb = x_ref[...].astype(BF16)
    col = 0
    for o_ref, width, scale in zip(out_refs, widths, scales):
        r = jnp.dot(xb, w_ref[:, col:col + width], preferred_element_type=F32)
        if scale != 1.0:
            r = r * scale
        o_ref[...] = r.astype(o_ref.dtype)
        col += width


def _inproj(x2, w_bf16, widths, dtypes, scales, *, tm, name):
    t, d = x2.shape
    n = w_bf16.shape[1]
    assert sum(widths) == n and t % tm == 0
    out_bytes = sum(tm * wd * jnp.dtype(dt).itemsize for wd, dt in zip(widths, dtypes))
    vmem = d * n * 2 + 2 * tm * d * 4 + 2 * out_bytes + tm * d * 2 + 3 * tm * max(widths) * 4
    return pl.pallas_call(
        functools.partial(_inproj_kernel, widths=tuple(widths), scales=tuple(scales)),
        grid=(t // tm,),
        in_specs=[
            pl.BlockSpec((tm, d), lambda i: (i, 0)),
            pl.BlockSpec((d, n), lambda i: (0, 0), pipeline_mode=pl.Buffered(1)),
        ],
        out_specs=[pl.BlockSpec((tm, wd), lambda i: (i, 0)) for wd in widths],
        out_shape=[jax.ShapeDtypeStruct((t, wd), dt) for wd, dt in zip(widths, dtypes)],
        compiler_params=pltpu.CompilerParams(
            dimension_semantics=("arbitrary",), vmem_limit_bytes=_vmem_limit(vmem + (8 << 20))),
        name=name,
    )(x2, w_bf16)


def _outproj_ln_kernel(*refs, n_y, alpha):
    y_refs = refs[:n_y]
    w_ref, x_ref, g_ref, b_ref, o_ref = refs[n_y:]
    acc = alpha * x_ref[...]
    row = 0
    for y_ref in y_refs:
        kdim = y_ref.shape[1]
        acc = acc + jnp.dot(y_ref[...], w_ref[row:row + kdim, :], preferred_element_type=F32)
        row += kdim
    mu = jnp.mean(acc, axis=-1, keepdims=True)
    dev = acc - mu
    var = jnp.mean(dev * dev, axis=-1, keepdims=True)
    o_ref[...] = dev * lax.rsqrt(var + LN_EPS) * g_ref[...] + b_ref[...]


def _outproj_ln(ys, w_bf16, x2, ln_g, ln_b, alpha, *, tm, name):
    t, d = x2.shape
    kdim = w_bf16.shape[0]
    assert sum(y.shape[1] for y in ys) == kdim and t % tm == 0
    vmem = kdim * d * 2 + 2 * tm * kdim * 2 + 4 * tm * d * 4 + 4 * tm * d * 4
    return pl.pallas_call(
        functools.partial(_outproj_ln_kernel, n_y=len(ys), alpha=alpha),
        grid=(t // tm,),
        in_specs=(
            [pl.BlockSpec((tm, y.shape[1]), lambda i: (i, 0)) for y in ys]
            + [
                pl.BlockSpec((kdim, d), lambda i: (0, 0), pipeline_mode=pl.Buffered(1)),
                pl.BlockSpec((tm, d), lambda i: (i, 0)),
                pl.BlockSpec((1, d), lambda i: (0, 0)),
                pl.BlockSpec((1, d), lambda i: (0, 0)),
            ]
        ),
        out_specs=pl.BlockSpec((tm, d), lambda i: (i, 0)),
        out_shape=jax.ShapeDtypeStruct((t, d), F32),
        compiler_params=pltpu.CompilerParams(
            dimension_semantics=("arbitrary",), vmem_limit_bytes=_vmem_limit(vmem + (8 << 20))),
        name=name,
    )(*ys, w_bf16, x2, ln_g.reshape(1, d), ln_b.reshape(1, d))


def _rglru_kernel(ax_ref, ag_ref, cw_ref, cb_ref, wg_ref, ba_ref, bx_ref, lam_ref, y_ref,
                  halo_ref, h_ref, *, tc):
    @pl.when(pl.program_id(2) == 0)
    def _():
        halo_ref[...] = jnp.zeros_like(halo_ref)
        h_ref[...] = jnp.zeros_like(h_ref)

    x = ax_ref[...]
    halo = halo_ref[...]
    row8 = lax.broadcasted_iota(jnp.int32, (SUBLANES, LANES), 0)
    cw = cw_ref[...]
    xc = cb_ref[...] + cw[CONV_WIDTH - 1:CONV_WIDTH] * x
    for d in range(1, CONV_WIDTH):
        rolled = pltpu.roll(x, d, axis=0)
        head = jnp.where(row8 < d, pltpu.roll(halo, d, axis=0), rolled[:SUBLANES])
        xd = jnp.concatenate([head, rolled[SUBLANES:]], axis=0)
        xc = xc + cw[CONV_WIDTH - 1 - d:CONV_WIDTH - d] * xd
    halo_ref[...] = x[tc - SUBLANES:]

    gates = jnp.dot(xc.astype(BF16), wg_ref[0], preferred_element_type=F32)
    r = _sigmoid(gates[:, :LRU_BLOCK] + ba_ref[...])
    i = _sigmoid(gates[:, LRU_BLOCK:] + bx_ref[...])
    lam = lam_ref[...]
    log_sig_lam = jnp.minimum(lam, 0.0) - jnp.log(1.0 + jnp.exp(-jnp.abs(lam)))
    log_a = (LRU_C * log_sig_lam) * r
    a = jnp.exp(log_a)
    u = jnp.sqrt(1.0 - a * a) * (i * xc)

    rows = lax.broadcasted_iota(jnp.int32, (tc, LANES), 0)
    d = 1
    while d < tc:
        if d < SUBLANES:
            a_sh = jnp.where(rows >= d, pltpu.roll(a, d, axis=0), 1.0)
            u_sh = jnp.where(rows >= d, pltpu.roll(u, d, axis=0), 0.0)
            u = u + a * u_sh
            a = a * a_sh
        else:
            u_tail = u[d:] + a[d:] * u[:tc - d]
            a_tail = a[d:] * a[:tc - d]
            u = jnp.concatenate([u[:d], u_tail], axis=0)
            a = jnp.concatenate([a[:d], a_tail], axis=0)
        d *= 2
    h = u + a * h_ref[...]
    h_ref[...] = h[tc - 1:tc]
    y_ref[...] = (h * _silu(ag_ref[...].astype(F32))).astype(y_ref.dtype)


def _rglru(a_x, a_g, conv_w, conv_b, w_gates_bf16, b_a, b_x, lam, *, batch, tc):
    t, width = a_x.shape
    seq = t // batch
    nblk = width // LRU_BLOCK
    nt = seq // tc
    assert seq % tc == 0
    row_blk = lambda b, n, s: (b * nt + s, n)
    par_blk = lambda b, n, s: (0, n)
    return pl.pallas_call(
        functools.partial(_rglru_kernel, tc=tc),
        grid=(batch, nblk, nt),
        in_specs=[
            pl.BlockSpec((tc, LRU_BLOCK), row_blk),
            pl.BlockSpec((tc, LRU_BLOCK), row_blk),
            pl.BlockSpec((CONV_WIDTH, LRU_BLOCK), par_blk),
            pl.BlockSpec((1, LRU_BLOCK), par_blk),
            pl.BlockSpec((1, LRU_BLOCK, 2 * LRU_BLOCK), lambda b, n, s: (n, 0, 0)),
            pl.BlockSpec((1, LRU_BLOCK), par_blk),
            pl.BlockSpec((1, LRU_BLOCK), par_blk),
            pl.BlockSpec((1, LRU_BLOCK), par_blk),
        ],
        out_specs=pl.BlockSpec((tc, LRU_BLOCK), row_blk),
        out_shape=jax.ShapeDtypeStruct((t, width), BF16),
        scratch_shapes=[pltpu.VMEM((SUBLANES, LRU_BLOCK), F32), pltpu.VMEM((1, LRU_BLOCK), F32)],
        compiler_params=pltpu.CompilerParams(
            dimension_semantics=("arbitrary", "arbitrary", "arbitrary"),
            vmem_limit_bytes=_vmem_limit(24 << 20)),
        name="rglru",
    )(a_x, a_g, conv_w, conv_b.reshape(1, width), w_gates_bf16, b_a.reshape(1, width),
      b_x.reshape(1, width), lam.reshape(1, width))


def _stickbreak_kernel(q_ref, k_ref, v_ref, g_ref, o_ref, tri_ref, *, tq, tk):
    i = pl.program_id(2)

    @pl.when((pl.program_id(0) == 0) & (pl.program_id(1) == 0) & (i == 0))
    def _():
        r = lax.broadcasted_iota(jnp.int32, (tk, tk), 0)
        c = lax.broadcasted_iota(jnp.int32, (tk, tk), 1)
        tri_ref[...] = jnp.where(r > c, 1.0, 0.0).astype(BF16)

    q = q_ref[...]
    tri = tri_ref[...]
    nsub = tq // tk

    def block(j, acc, carry, masked):
        start = pl.multiple_of(j * tk, tk)
        kb = k_ref[pl.ds(start, tk), :]
        vb = v_ref[pl.ds(start, tk), :]
        z = lax.dot_general(q, kb, (((1,), (1,)), ((), ())), preferred_element_type=F32)
        softplus = jnp.maximum(z, 0.0) + jnp.log(1.0 + jnp.exp(-jnp.abs(z)))
        if masked:
            rows = i * tq + lax.broadcasted_iota(jnp.int32, (tq, tk), 0)
            cols = j * tk + lax.broadcasted_iota(jnp.int32, (tq, tk), 1)
            causal = cols < rows
            l1mb = jnp.where(causal, -softplus, 0.0)
        else:
            l1mb = -softplus
        hi = l1mb.astype(BF16)
        lo = (l1mb - hi.astype(F32)).astype(BF16)
        suffix = (jnp.dot(hi, tri, preferred_element_type=F32)
                  + jnp.dot(lo, tri, preferred_element_type=F32))
        w = jnp.exp(z + l1mb + suffix + carry)
        if masked:
            w = jnp.where(causal, w, 0.0)
        acc = acc + jnp.dot(w.astype(BF16), vb, preferred_element_type=F32)
        carry = carry + jnp.sum(l1mb, axis=-1, keepdims=True)
        return acc, carry

    acc = jnp.zeros((tq, q.shape[1]), F32)
    carry = jnp.zeros((tq, 1), F32)
    for s in range(nsub):
        acc, carry = block(i * nsub + (nsub - 1 - s), acc, carry, True)

    def body(jj, state):
        return block(i * nsub - 1 - jj, state[0], state[1], False)

    acc, carry = lax.fori_loop(0, i * nsub, body, (acc, carry))
    o_ref[...] = (acc * _silu(g_ref[...].astype(F32))).astype(o_ref.dtype)


def _stickbreak(q, k, v, g, *, batch, tq, tk):
    t, width = q.shape
    seq = t // batch
    heads = width // SB_HEAD_DIM
    nq = seq // tq
    assert seq % tq == 0 and tq % tk == 0
    q_blk = lambda b, h, i: (b * nq + i, h)
    kv_blk = lambda b, h, i: (b, h)
    vmem = 2 * 2 * seq * SB_HEAD_DIM * 2 + 16 * tq * tk * 4
    return pl.pallas_call(
        functools.partial(_stickbreak_kernel, tq=tq, tk=tk),
        grid=(batch, heads, nq),
        in_specs=[
            pl.BlockSpec((tq, SB_HEAD_DIM), q_blk),
            pl.BlockSpec((seq, SB_HEAD_DIM), kv_blk),
            pl.BlockSpec((seq, SB_HEAD_DIM), kv_blk),
            pl.BlockSpec((tq, SB_HEAD_DIM), q_blk),
        ],
        out_specs=pl.BlockSpec((tq, SB_HEAD_DIM), q_blk),
        out_shape=jax.ShapeDtypeStruct((t, width), BF16),
        scratch_shapes=[pltpu.VMEM((tk, tk), BF16)],
        compiler_params=pltpu.CompilerParams(
            dimension_semantics=("arbitrary", "arbitrary", "arbitrary"),
            vmem_limit_bytes=_vmem_limit(vmem + (8 << 20))),
        name="stickbreak",
    )(q, k, v, g)


def _alibi_slope(h):
    return 2.0 ** (-8.0 * (h + 1) / C_HEADS)


def _swa_kernel(sink_ref, q_ref, kp_ref, kc_ref, vp_ref, vc_ref, g_ref, o_ref, *, tq, blocks_per_seq):
    nk = WINDOW + tq
    pairs = C_GROUP // 2
    first = (pl.program_id(0) % blocks_per_seq) == 0

    r = lax.broadcasted_iota(jnp.int32, (tq, nk), 0)
    c = lax.broadcasted_iota(jnp.int32, (tq, nk), 1)
    dist = r - c + WINDOW
    first_key = jnp.where(first, WINDOW, 0)
    valid = (dist >= 0) & (dist < WINDOW) & (c >= first_key)
    mdist = jnp.where(valid, dist.astype(F32), jnp.inf)

    lane = lax.broadcasted_iota(jnp.int32, (nk, LANES), 1)
    low = lane < C_HEAD_DIM

    def halves(prev_ref, cur_ref):
        full = jnp.concatenate([prev_ref[...], cur_ref[...]], axis=0).astype(F32)
        swapped = pltpu.roll(full, C_HEAD_DIM, axis=1)
        zero = jnp.zeros_like(full)
        even = [jnp.where(low, full, zero), jnp.where(low, swapped, zero)]
        odd = [jnp.where(low, zero, swapped), jnp.where(low, zero, full)]
        return [[e.astype(BF16) for e in even], [o.astype(BF16) for o in odd]]

    k_ext = halves(kp_ref, kc_ref)
    v_ext = halves(vp_ref, vc_ref)

    for kv in range(C_KV_HEADS):
        cols = [(kv * pairs + p) * LANES for p in range(pairs)]
        qs = jnp.concatenate([q_ref[:, c0:c0 + LANES] for c0 in cols], axis=0)
        out = None
        for parity in range(2):
            s = lax.dot_general(qs, k_ext[parity][kv], (((1,), (1,)), ((), ())),
                                preferred_element_type=F32)
            probs, invs = [], []
            for p in range(pairs):
                head = kv * C_GROUP + 2 * p + parity
                sink = sink_ref[head]
                sc = s[p * tq:(p + 1) * tq] - _alibi_slope(head) * mdist
                m = jnp.maximum(jnp.max(sc, axis=-1, keepdims=True), sink)
                pe = jnp.exp(sc - m)
                denom = jnp.sum(pe, axis=-1, keepdims=True) + jnp.exp(sink - m)
                probs.append(pe.astype(BF16))
                invs.append(1.0 / denom)
            o_par = jnp.dot(jnp.concatenate(probs, axis=0), v_ext[parity][kv],
                            preferred_element_type=F32)
            o_par = o_par * jnp.concatenate(invs, axis=0)
            out = o_par if out is None else out + o_par
        for p, c0 in enumerate(cols):
            gate = _silu(g_ref[:, c0:c0 + LANES].astype(F32))
            o_ref[:, c0:c0 + LANES] = (out[p * tq:(p + 1) * tq] * gate).astype(o_ref.dtype)


def _swa(q, k, v, g, sinks, *, batch, tq):
    t, width = q.shape
    seq = t // batch
    assert seq % tq == 0 and tq % WINDOW == 0 and k.shape[1] == LANES
    blocks_per_seq = seq // tq
    ratio = tq // WINDOW
    cur = lambda i: (i, 0)
    prev = lambda i: (jnp.maximum(i * ratio - 1, 0), 0)
    return pl.pallas_call(
        functools.partial(_swa_kernel, tq=tq, blocks_per_seq=blocks_per_seq),
        grid=(t // tq,),
        in_specs=[
            pl.BlockSpec(memory_space=pltpu.SMEM),
            pl.BlockSpec((tq, width), cur),
            pl.BlockSpec((WINDOW, LANES), prev),
            pl.BlockSpec((tq, LANES), cur),
            pl.BlockSpec((WINDOW, LANES), prev),
            pl.BlockSpec((tq, LANES), cur),
            pl.BlockSpec((tq, width), cur),
        ],
        out_specs=pl.BlockSpec((tq, width), cur),
        out_shape=jax.ShapeDtypeStruct((t, width), BF16),
        compiler_params=pltpu.CompilerParams(
            dimension_semantics=("arbitrary",), vmem_limit_bytes=_vmem_limit(32 << 20)),
        name="swa",
    )(sinks, q, k, k, v, v, g)


def _even_layer(x2, batch, w_in, conv_w, conv_b, w_gate_a, b_gate_a, w_gate_x, b_gate_x, lam,
                w_out, ln_g, ln_b, alpha):
    d = x2.shape[1]
    a_x, a_g, b_q, b_k, b_v, b_g = _inproj(
        x2, w_in.astype(BF16), widths=[d] * 6, dtypes=[F32, BF16, BF16, BF16, BF16, BF16],
        scales=[1.0, 1.0, 1.0 / math.sqrt(SB_HEAD_DIM), 1.0, 1.0, 1.0], tm=512, name="inproj_even")
    w_gates = jnp.concatenate([w_gate_a, w_gate_x], axis=-1).astype(BF16)
    y_a = _rglru(a_x, a_g, conv_w, conv_b, w_gates, b_gate_a, b_gate_x, lam, batch=batch, tc=256)
    y_b = _stickbreak(b_q, b_k, b_v, b_g, batch=batch, tq=512, tk=256)
    return _outproj_ln([y_a, y_b], w_out.astype(BF16), x2, ln_g, ln_b, alpha, tm=512,
                       name="outproj_ln_even")


def _odd_layer(x2, batch, w_in, sinks, w_out, ln_g, ln_b, alpha):
    d = x2.shape[1]
    kv_width = C_KV_HEADS * C_HEAD_DIM
    q, k, v, g = _inproj(
        x2, w_in.astype(BF16), widths=[d, kv_width, kv_width, d], dtypes=[BF16] * 4,
        scales=[1.0 / math.sqrt(C_HEAD_DIM), 1.0, 1.0, 1.0], tm=512, name="inproj_odd")
    y = _swa(q, k, v, g, sinks, batch=batch, tq=WINDOW)
    return _outproj_ln([y], w_out.astype(BF16), x2, ln_g, ln_b, alpha, tm=512, name="outproj_ln_odd")


def kernel(x, e_w_in, e_conv_w, e_conv_b, e_w_gate_a, e_b_gate_a, e_w_gate_x, e_b_gate_x, e_lru_lambda, e_w_out, e_ln_g, e_ln_b, o_w_in, o_sinks, o_w_out, o_ln_g, o_ln_b):
    batch, seq, d = x.shape
    depth = e_w_in.shape[0] + o_w_in.shape[0]
    alpha = float((2 * depth) ** 0.25)
    x2 = x.reshape(batch * seq, d)
    for layer in range(depth):
        li = layer // 2
        if layer % 2 == 0:
            x2 = _even_layer(x2, batch, e_w_in[li], e_conv_w[li], e_conv_b[li], e_w_gate_a[li],
                             e_b_gate_a[li], e_w_gate_x[li], e_b_gate_x[li], e_lru_lambda[li],
                             e_w_out[li], e_ln_g[li], e_ln_b[li], alpha)
        else:
            x2 = _odd_layer(x2, batch, o_w_in[li], o_sinks[li], o_w_out[li], o_ln_g[li], o_ln_b[li],
                            alpha)
    return x2.reshape(batch, seq, d)
```

```python
import functools
import math

import jax
import jax.numpy as jnp
from jax import lax
from jax.experimental import pallas as pl
from jax.experimental.pallas import tpu as pltpu

F32 = jnp.float32
BF16 = jnp.bfloat16

LANES = 128
SUBLANES = 8
VMEM_BYTES_V7X = 64 * 1024 * 1024

LRU_BLOCK = 128
LRU_C = 8.0
CONV_WIDTH = 4
SB_HEAD_DIM = 128
C_HEADS = 16
C_KV_HEADS = 2
C_GROUP = C_HEADS // C_KV_HEADS
C_HEAD_DIM = 64
WINDOW = 128
LN_EPS = 1e-5
LOG2E = math.log2(math.e)


def _sigmoid(x):
    return 0.5 * jnp.tanh(0.5 * x) + 0.5


def _silu(x):
    return x * _sigmoid(x)


def _vmem_limit(nbytes):
    return int(min(nbytes, VMEM_BYTES_V7X - 8 * 1024 * 1024))


def _inproj_kernel(x_ref, w_ref, *out_refs, widths, scales):
    xb = x_ref[...].astype(BF16)
    col = 0
    for o_ref, width, scale in zip(out_refs, widths, scales):
        r = jnp.dot(xb, w_ref[:, col:col + width], preferred_element_type=F32)
        if scale != 1.0:
            r = r * scale
        o_ref[...] = r.astype(o_ref.dtype)
        col += width


def _inproj(x2, w_bf16, widths, dtypes, scales, *, tm, name):
    t, d = x2.shape
    n = w_bf16.shape[1]
    assert sum(widths) == n and t % tm == 0
    out_bytes = sum(tm * wd * jnp.dtype(dt).itemsize for wd, dt in zip(widths, dtypes))
    vmem = d * n * 2 + 2 * tm * d * 4 + 2 * out_bytes + tm * d * 2 + 3 * tm * max(widths) * 4
    return pl.pallas_call(
        functools.partial(_inproj_kernel, widths=tuple(widths), scales=tuple(scales)),
        grid=(t // tm,),
        in_specs=[
            pl.BlockSpec((tm, d), lambda i: (i, 0)),
            pl.BlockSpec((d, n), lambda i: (0, 0), pipeline_mode=pl.Buffered(1)),
        ],
        out_specs=[pl.BlockSpec((tm, wd), lambda i: (i, 0)) for wd in widths],
        out_shape=[jax.ShapeDtypeStruct((t, wd), dt) for wd, dt in zip(widths, dtypes)],
        compiler_params=pltpu.CompilerParams(
            dimension_semantics=("arbitrary",), vmem_limit_bytes=_vmem_limit(vmem + (8 << 20))),
        name=name,
    )(x2, w_bf16)


def _outproj_ln_kernel(*refs, n_y, alpha):
    y_refs = refs[:n_y]
    w_ref, x_ref, g_ref, b_ref, o_ref = refs[n_y:]
    acc = alpha * x_ref[...]
    row = 0
    for y_ref in y_refs:
        kdim = y_ref.shape[1]
        acc = acc + jnp.dot(y_ref[...], w_ref[row:row + kdim, :], preferred_element_type=F32)
        row += kdim
    mu = jnp.mean(acc, axis=-1, keepdims=True)
    dev = acc - mu
    var = jnp.mean(dev * dev, axis=-1, keepdims=True)
    o_ref[...] = dev * lax.rsqrt(var + LN_EPS) * g_ref[...] + b_ref[...]


def _outproj_ln(ys, w_bf16, x2, ln_g, ln_b, alpha, *, tm, name):
    t, d = x2.shape
    kdim = w_bf16.shape[0]
    assert sum(y.shape[1] for y in ys) == kdim and t % tm == 0
    vmem = kdim * d * 2 + 2 * tm * kdim * 2 + 4 * tm * d * 4 + 4 * tm * d * 4
    return pl.pallas_call(
        functools.partial(_outproj_ln_kernel, n_y=len(ys), alpha=alpha),
        grid=(t // tm,),
        in_specs=(
            [pl.BlockSpec((tm, y.shape[1]), lambda i: (i, 0)) for y in ys]
            + [
                pl.BlockSpec((kdim, d), lambda i: (0, 0), pipeline_mode=pl.Buffered(1)),
                pl.BlockSpec((tm, d), lambda i: (i, 0)),
                pl.BlockSpec((1, d), lambda i: (0, 0)),
                pl.BlockSpec((1, d), lambda i: (0, 0)),
            ]
        ),
        out_specs=pl.BlockSpec((tm, d), lambda i: (i, 0)),
        out_shape=jax.ShapeDtypeStruct((t, d), F32),
        compiler_params=pltpu.CompilerParams(
            dimension_semantics=("arbitrary",), vmem_limit_bytes=_vmem_limit(vmem + (8 << 20))),
        name=name,
    )(*ys, w_bf16, x2, ln_g.reshape(1, d), ln_b.reshape(1, d))


def _rglru_kernel(ax_ref, ag_ref, cw_ref, cb_ref, wg_ref, ba_ref, bx_ref, lam_ref, y_ref,
                  halo_ref, h_ref, *, tc):
    @pl.when(pl.program_id(2) == 0)
    def _():
        halo_ref[...] = jnp.zeros_like(halo_ref)
        h_ref[...] = jnp.zeros_like(h_ref)

    x = ax_ref[...]
    halo = halo_ref[...]
    row8 = lax.broadcasted_iota(jnp.int32, (SUBLANES, LANES), 0)
    cw = cw_ref[...]
    xc = cb_ref[...] + cw[CONV_WIDTH - 1:CONV_WIDTH] * x
    for d in range(1, CONV_WIDTH):
        rolled = pltpu.roll(x, d, axis=0)
        head = jnp.where(row8 < d, pltpu.roll(halo, d, axis=0), rolled[:SUBLANES])
        xd = jnp.concatenate([head, rolled[SUBLANES:]], axis=0)
        xc = xc + cw[CONV_WIDTH - 1 - d:CONV_WIDTH - d] * xd
    halo_ref[...] = x[tc - SUBLANES:]

    gates = jnp.dot(xc.astype(BF16), wg_ref[0], preferred_element_type=F32)
    r = _sigmoid(gates[:, :LRU_BLOCK] + ba_ref[...])
    i = _sigmoid(gates[:, LRU_BLOCK:] + bx_ref[...])
    lam = lam_ref[...]
    log_sig_lam = jnp.minimum(lam, 0.0) - jnp.log(1.0 + jnp.exp(-jnp.abs(lam)))
    log_a = (LRU_C * log_sig_lam) * r
    a = jnp.exp(log_a)
    u = jnp.sqrt(1.0 - a * a) * (i * xc)

    rows = lax.broadcasted_iota(jnp.int32, (tc, LANES), 0)
    d = 1
    while d < tc:
        if d < SUBLANES:
            a_sh = jnp.where(rows >= d, pltpu.roll(a, d, axis=0), 1.0)
            u_sh = jnp.where(rows >= d, pltpu.roll(u, d, axis=0), 0.0)
            u = u + a * u_sh
            a = a * a_sh
        else:
            u_tail = u[d:] + a[d:] * u[:tc - d]
            a_tail = a[d:] * a[:tc - d]
            u = jnp.concatenate([u[:d], u_tail], axis=0)
            a = jnp.concatenate([a[:d], a_tail], axis=0)
        d *= 2
    h = u + a * h_ref[...]
    h_ref[...] = h[tc - 1:tc]
    y_ref[...] = (h * _silu(ag_ref[...].astype(F32))).astype(y_ref.dtype)


def _rglru(a_x, a_g, conv_w, conv_b, w_gates_bf16, b_a, b_x, lam, *, batch, tc):
    t, width = a_x.shape
    seq = t // batch
    nblk = width // LRU_BLOCK
    nt = seq // tc
    assert seq % tc == 0
    row_blk = lambda b, n, s: (b * nt + s, n)
    par_blk = lambda b, n, s: (0, n)
    return pl.pallas_call(
        functools.partial(_rglru_kernel, tc=tc),
        grid=(batch, nblk, nt),
        in_specs=[
            pl.BlockSpec((tc, LRU_BLOCK), row_blk),
            pl.BlockSpec((tc, LRU_BLOCK), row_blk),
            pl.BlockSpec((CONV_WIDTH, LRU_BLOCK), par_blk),
            pl.BlockSpec((1, LRU_BLOCK), par_blk),
            pl.BlockSpec((1, LRU_BLOCK, 2 * LRU_BLOCK), lambda b, n, s: (n, 0, 0)),
            pl.BlockSpec((1, LRU_BLOCK), par_blk),
            pl.BlockSpec((1, LRU_BLOCK), par_blk),
            pl.BlockSpec((1, LRU_BLOCK), par_blk),
        ],
        out_specs=pl.BlockSpec((tc, LRU_BLOCK), row_blk),
        out_shape=jax.ShapeDtypeStruct((t, width), BF16),
        scratch_shapes=[pltpu.VMEM((SUBLANES, LRU_BLOCK), F32), pltpu.VMEM((1, LRU_BLOCK), F32)],
        compiler_params=pltpu.CompilerParams(
            dimension_semantics=("arbitrary", "arbitrary", "arbitrary"),
            vmem_limit_bytes=_vmem_limit(24 << 20)),
        name="rglru",
    )(a_x, a_g, conv_w, conv_b.reshape(1, width), w_gates_bf16, b_a.reshape(1, width),
      b_x.reshape(1, width), lam.reshape(1, width))


def _stickbreak_kernel(q_ref, k_ref, v_ref, g_ref, o_ref, ntri_ref, *, tq, tk, nh):
    dh = SB_HEAD_DIM
    i = pl.program_id(2)

    @pl.when((pl.program_id(0) == 0) & (pl.program_id(1) == 0) & (i == 0))
    def _():
        r = lax.broadcasted_iota(jnp.int32, (tk, tk), 0)
        c = lax.broadcasted_iota(jnp.int32, (tk, tk), 1)
        ntri_ref[...] = jnp.where(r >= c, -1.0, 0.0).astype(BF16)

    ntri2 = jnp.concatenate([ntri_ref[...], ntri_ref[...]], axis=0)
    nsub = tq // tk

    def blocks(j_first, state, masked):
        accs = [state[2 * h] for h in range(nh)]
        carries = [state[2 * h + 1] for h in range(nh)]
        chains = [(s, h) for s in range(nsub) for h in range(nh)]
        zs, causal = {}, {}
        for s, h in chains:
            j = j_first - s
            start = pl.multiple_of(j * tk, tk)
            q = q_ref[:, h * dh:(h + 1) * dh]
            kb = k_ref[pl.ds(start, tk), h * dh:(h + 1) * dh]
            zs[s, h] = lax.dot_general(q, kb, (((1,), (1,)), ((), ())), preferred_element_type=F32)
            if masked and h == 0:
                rows = i * tq + lax.broadcasted_iota(jnp.int32, (tq, tk), 0)
                cols = j * tk + lax.broadcasted_iota(jnp.int32, (tq, tk), 1)
                causal[s] = cols < rows
        logits = {}
        for s, h in chains:
            z = zs[s, h]
            sp = jnp.maximum(z, 0.0) + jnp.log(1.0 + jnp.exp2(-jnp.abs(z))) * LOG2E
            if masked:
                sp = jnp.where(causal[s], sp, 0.0)
            hi = sp.astype(BF16)
            lo = (sp - hi.astype(F32)).astype(BF16)
            suffix = jnp.dot(jnp.concatenate([hi, lo], axis=1), ntri2, preferred_element_type=F32)
            logits[s, h] = z + suffix + carries[h]
            carries[h] = carries[h] - jnp.sum(sp, axis=-1, keepdims=True)
        for s, h in chains:
            start = pl.multiple_of((j_first - s) * tk, tk)
            vb = v_ref[pl.ds(start, tk), h * dh:(h + 1) * dh]
            w = jnp.exp2(logits[s, h])
            if masked:
                w = jnp.where(causal[s], w, 0.0)
            accs[h] = accs[h] + jnp.dot(w.astype(BF16), vb, preferred_element_type=F32)
        return tuple(x for h in range(nh) for x in (accs[h], carries[h]))

    state = tuple(jnp.zeros((tq, dh if s % 2 == 0 else 1), F32) for s in range(2 * nh))
    state = blocks(i * nsub + nsub - 1, state, True)
    state = lax.fori_loop(
        0, i, lambda jj, st: blocks((i - jj) * nsub - 1, st, False), state)
    for h in range(nh):
        gate = _silu(g_ref[:, h * dh:(h + 1) * dh].astype(F32))
        o_ref[:, h * dh:(h + 1) * dh] = (state[2 * h] * gate).astype(o_ref.dtype)


def _stickbreak(q, k, v, g, *, batch, tq, tk, nh):
    t, width = q.shape
    seq = t // batch
    gw = nh * SB_HEAD_DIM
    groups = width // gw
    nq = seq // tq
    assert seq % tq == 0 and tq % tk == 0 and width % gw == 0
    q_blk = lambda b, h, i: (b * nq + i, h)
    kv_blk = lambda b, h, i: (b, h)
    vmem = 2 * 2 * seq * gw * 2 + nh * 16 * tq * tk * 4
    return pl.pallas_call(
        functools.partial(_stickbreak_kernel, tq=tq, tk=tk, nh=nh),
        grid=(batch, groups, nq),
        in_specs=[
            pl.BlockSpec((tq, gw), q_blk),
            pl.BlockSpec((seq, gw), kv_blk),
            pl.BlockSpec((seq, gw), kv_blk),
            pl.BlockSpec((tq, gw), q_blk),
        ],
        out_specs=pl.BlockSpec((tq, gw), q_blk),
        out_shape=jax.ShapeDtypeStruct((t, width), BF16),
        scratch_shapes=[pltpu.VMEM((tk, tk), BF16)],
        compiler_params=pltpu.CompilerParams(
            dimension_semantics=("arbitrary", "arbitrary", "arbitrary"),
            vmem_limit_bytes=_vmem_limit(vmem + (8 << 20))),
        name="stickbreak",
    )(q, k, v, g)


def _alibi_slope(h):
    return 2.0 ** (-8.0 * (h + 1) / C_HEADS)


def _swa_kernel(sink_ref, q_ref, kp_ref, kc_ref, vp_ref, vc_ref, g_ref, o_ref, *, tq, blocks_per_seq):
    nk = WINDOW + tq
    pairs = C_GROUP // 2
    first = (pl.program_id(0) % blocks_per_seq) == 0

    r = lax.broadcasted_iota(jnp.int32, (tq, nk), 0)
    c = lax.broadcasted_iota(jnp.int32, (tq, nk), 1)
    dist = r - c + WINDOW
    first_key = jnp.where(first, WINDOW, 0)
    valid = (dist >= 0) & (dist < WINDOW) & (c >= first_key)
    mdist = jnp.where(valid, dist.astype(F32), jnp.inf)

    lane = lax.broadcasted_iota(jnp.int32, (nk, LANES), 1)
    low = lane < C_HEAD_DIM

    def halves(prev_ref, cur_ref):
        full = jnp.concatenate([prev_ref[...], cur_ref[...]], axis=0).astype(F32)
        swapped = pltpu.roll(full, C_HEAD_DIM, axis=1)
        zero = jnp.zeros_like(full)
        even = [jnp.where(low, full, zero), jnp.where(low, swapped, zero)]
        odd = [jnp.where(low, zero, swapped), jnp.where(low, zero, full)]
        return [[e.astype(BF16) for e in even], [o.astype(BF16) for o in odd]]

    k_ext = halves(kp_ref, kc_ref)
    v_ext = halves(vp_ref, vc_ref)

    for kv in range(C_KV_HEADS):
        cols = [(kv * pairs + p) * LANES for p in range(pairs)]
        qs = jnp.concatenate([q_ref[:, c0:c0 + LANES] for c0 in cols], axis=0)
        out = None
        for parity in range(2):
            s = lax.dot_general(qs, k_ext[parity][kv], (((1,), (1,)), ((), ())),
                                preferred_element_type=F32)
            probs, invs = [], []
            for p in range(pairs):
                head = kv * C_GROUP + 2 * p + parity
                sink = sink_ref[head]
                sc = s[p * tq:(p + 1) * tq] - _alibi_slope(head) * mdist
                m = jnp.maximum(jnp.max(sc, axis=-1, keepdims=True), sink)
                pe = jnp.exp(sc - m)
                denom = jnp.sum(pe, axis=-1, keepdims=True) + jnp.exp(sink - m)
                probs.append(pe.astype(BF16))
                invs.append(1.0 / denom)
            o_par = jnp.dot(jnp.concatenate(probs, axis=0), v_ext[parity][kv],
                            preferred_element_type=F32)
            o_par = o_par * jnp.concatenate(invs, axis=0)
            out = o_par if out is None else out + o_par
        for p, c0 in enumerate(cols):
            gate = _silu(g_ref[:, c0:c0 + LANES].astype(F32))
            o_ref[:, c0:c0 + LANES] = (out[p * tq:(p + 1) * tq] * gate).astype(o_ref.dtype)


def _swa(q, k, v, g, sinks, *, batch, tq):
    t, width = q.shape
    seq = t // batch
    assert seq % tq == 0 and tq % WINDOW == 0 and k.shape[1] == LANES
    blocks_per_seq = seq // tq
    ratio = tq // WINDOW
    cur = lambda i: (i, 0)
    prev = lambda i: (jnp.maximum(i * ratio - 1, 0), 0)
    return pl.pallas_call(
        functools.partial(_swa_kernel, tq=tq, blocks_per_seq=blocks_per_seq),
        grid=(t // tq,),
        in_specs=[
            pl.BlockSpec(memory_space=pltpu.SMEM),
            pl.BlockSpec((tq, width), cur),
            pl.BlockSpec((WINDOW, LANES), prev),
            pl.BlockSpec((tq, LANES), cur),
            pl.BlockSpec((WINDOW, LANES), prev),
            pl.BlockSpec((tq, LANES), cur),
            pl.BlockSpec((tq, width), cur),
        ],
        out_specs=pl.BlockSpec((tq, width), cur),
        out_shape=jax.ShapeDtypeStruct((t, width), BF16),
        compiler_params=pltpu.CompilerParams(
            dimension_semantics=("arbitrary",), vmem_limit_bytes=_vmem_limit(32 << 20)),
        name="swa",
    )(sinks, q, k, k, v, v, g)


def _even_layer(x2, batch, w_in, conv_w, conv_b, w_gate_a, b_gate_a, w_gate_x, b_gate_x, lam,
                w_out, ln_g, ln_b, alpha):
    d = x2.shape[1]
    a_x, a_g, b_q, b_k, b_v, b_g = _inproj(
        x2, w_in.astype(BF16), widths=[d] * 6, dtypes=[F32, BF16, BF16, BF16, BF16, BF16],
        scales=[1.0, 1.0, LOG2E / math.sqrt(SB_HEAD_DIM), 1.0, 1.0, 1.0], tm=512, name="inproj_even")
    w_gates = jnp.concatenate([w_gate_a, w_gate_x], axis=-1).astype(BF16)
    y_a = _rglru(a_x, a_g, conv_w, conv_b, w_gates, b_gate_a, b_gate_x, lam, batch=batch, tc=256)
    y_b = _stickbreak(b_q, b_k, b_v, b_g, batch=batch, tq=512, tk=256, nh=2)
    return _outproj_ln([y_a, y_b], w_out.astype(BF16), x2, ln_g, ln_b, alpha, tm=512,
                       name="outproj_ln_even")


def _odd_layer(x2, batch, w_in, sinks, w_out, ln_g, ln_b, alpha):
    d = x2.shape[1]
    kv_width = C_KV_HEADS * C_HEAD_DIM
    q, k, v, g = _inproj(
        x2, w_in.astype(BF16), widths=[d, kv_width, kv_width, d], dtypes=[BF16] * 4,
        scales=[1.0 / math.sqrt(C_HEAD_DIM), 1.0, 1.0, 1.0], tm=512, name="inproj_odd")
    y = _swa(q, k, v, g, sinks, batch=batch, tq=WINDOW)
    return _outproj_ln([y], w_out.astype(BF16), x2, ln_g, ln_b, alpha, tm=512, name="outproj_ln_odd")


def kernel(x, e_w_in, e_conv_w, e_conv_b, e_w_gate_a, e_b_gate_a, e_w_gate_x, e_b_gate_x, e_lru_lambda, e_w_out, e_ln_g, e_ln_b, o_w_in, o_sinks, o_w_out, o_ln_g, o_ln_b):
    batch, seq, d = x.shape
    depth = e_w_in.shape[0] + o_w_in.shape[0]
    alpha = float((2 * depth) ** 0.25)
    x2 = x.reshape(batch * seq, d)
    for layer in range(depth):
        li = layer // 2
        if layer % 2 == 0:
            x2 = _even_layer(x2, batch, e_w_in[li], e_conv_w[li], e_conv_b[li], e_w_gate_a[li],
                             e_b_gate_a[li], e_w_gate_x[li], e_b_gate_x[li], e_lru_lambda[li],
                             e_w_out[li], e_ln_g[li], e_ln_b[li], alpha)
        else:
            x2 = _odd_layer(x2, batch, o_w_in[li], o_sinks[li], o_w_out[li], o_ln_g[li], o_ln_b[li],
                            alpha)
    return x2.reshape(batch, seq, d)
```

```python
import functools
import math

import jax
import jax.numpy as jnp
from jax import lax
from jax.experimental import pallas as pl
from jax.experimental.pallas import tpu as pltpu

F32 = jnp.float32
BF16 = jnp.bfloat16

LANES = 128
SUBLANES = 8
VMEM_BYTES_V7X = 64 * 1024 * 1024

LRU_BLOCK = 128
LRU_C = 8.0
CONV_WIDTH = 4
SB_HEAD_DIM = 128
C_HEADS = 16
C_KV_HEADS = 2
C_GROUP = C_HEADS // C_KV_HEADS
C_HEAD_DIM = 64
WINDOW = 128
LN_EPS = 1e-5
LOG2E = math.log2(math.e)


def _sigmoid(x):
    return 0.5 * jnp.tanh(0.5 * x) + 0.5


def _silu(x):
    return x * _sigmoid(x)


def _vmem_limit(nbytes):
    return int(min(nbytes, VMEM_BYTES_V7X - 8 * 1024 * 1024))


def _inproj_kernel(x_ref, w_ref, *out_refs, widths, scales):
    xb = x_ref[...].astype(BF16)
    col = 0
    for o_ref, width, scale in zip(out_refs, widths, scales):
        r = jnp.dot(xb, w_ref[:, col:col + width], preferred_element_type=F32)
        if scale != 1.0:
            r = r * scale
        o_ref[...] = r.astype(o_ref.dtype)
        col += width


def _inproj(x2, w_bf16, widths, dtypes, scales, *, tm, name):
    t, d = x2.shape
    n = w_bf16.shape[1]
    assert sum(widths) == n and t % tm == 0
    out_bytes = sum(tm * wd * jnp.dtype(dt).itemsize for wd, dt in zip(widths, dtypes))
    vmem = d * n * 2 + 2 * tm * d * 4 + 2 * out_bytes + tm * d * 2 + 3 * tm * max(widths) * 4
    return pl.pallas_call(
        functools.partial(_inproj_kernel, widths=tuple(widths), scales=tuple(scales)),
        grid=(t // tm,),
        in_specs=[
            pl.BlockSpec((tm, d), lambda i: (i, 0)),
            pl.BlockSpec((d, n), lambda i: (0, 0), pipeline_mode=pl.Buffered(1)),
        ],
        out_specs=[pl.BlockSpec((tm, wd), lambda i: (i, 0)) for wd in widths],
        out_shape=[jax.ShapeDtypeStruct((t, wd), dt) for wd, dt in zip(widths, dtypes)],
        compiler_params=pltpu.CompilerParams(
            dimension_semantics=("arbitrary",), vmem_limit_bytes=_vmem_limit(vmem + (8 << 20))),
        name=name,
    )(x2, w_bf16)


def _outproj_ln_kernel(*refs, n_y, alpha):
    y_refs = refs[:n_y]
    w_ref, x_ref, g_ref, b_ref, o_ref = refs[n_y:]
    acc = alpha * x_ref[...]
    row = 0
    for y_ref in y_refs:
        kdim = y_ref.shape[1]
        acc = acc + jnp.dot(y_ref[...], w_ref[row:row + kdim, :], preferred_element_type=F32)
        row += kdim
    mu = jnp.mean(acc, axis=-1, keepdims=True)
    dev = acc - mu
    var = jnp.mean(dev * dev, axis=-1, keepdims=True)
    o_ref[...] = dev * lax.rsqrt(var + LN_EPS) * g_ref[...] + b_ref[...]


def _outproj_ln(ys, w_bf16, x2, ln_g, ln_b, alpha, *, tm, name):
    t, d = x2.shape
    kdim = w_bf16.shape[0]
    assert sum(y.shape[1] for y in ys) == kdim and t % tm == 0
    vmem = kdim * d * 2 + 2 * tm * kdim * 2 + 4 * tm * d * 4 + 4 * tm * d * 4
    return pl.pallas_call(
        functools.partial(_outproj_ln_kernel, n_y=len(ys), alpha=alpha),
        grid=(t // tm,),
        in_specs=(
            [pl.BlockSpec((tm, y.shape[1]), lambda i: (i, 0)) for y in ys]
            + [
                pl.BlockSpec((kdim, d), lambda i: (0, 0), pipeline_mode=pl.Buffered(1)),
                pl.BlockSpec((tm, d), lambda i: (i, 0)),
                pl.BlockSpec((1, d), lambda i: (0, 0)),
                pl.BlockSpec((1, d), lambda i: (0, 0)),
            ]
        ),
        out_specs=pl.BlockSpec((tm, d), lambda i: (i, 0)),
        out_shape=jax.ShapeDtypeStruct((t, d), F32),
        compiler_params=pltpu.CompilerParams(
            dimension_semantics=("arbitrary",), vmem_limit_bytes=_vmem_limit(vmem + (8 << 20))),
        name=name,
    )(*ys, w_bf16, x2, ln_g.reshape(1, d), ln_b.reshape(1, d))


def _rglru_kernel(ax_ref, ag_ref, cw_ref, cb_ref, wg_ref, ba_ref, bx_ref, lam_ref, y_ref,
                  halo_ref, h_ref, *, tc):
    @pl.when(pl.program_id(2) == 0)
    def _():
        halo_ref[...] = jnp.zeros_like(halo_ref)
        h_ref[...] = jnp.zeros_like(h_ref)

    x = ax_ref[...]
    width = x.shape[1]
    halo = halo_ref[...]
    row8 = lax.broadcasted_iota(jnp.int32, (SUBLANES, width), 0)
    cw = cw_ref[...]
    xc = cb_ref[...] + cw[CONV_WIDTH - 1:CONV_WIDTH] * x
    for d in range(1, CONV_WIDTH):
        rolled = pltpu.roll(x, d, axis=0)
        head = jnp.where(row8 < d, pltpu.roll(halo, d, axis=0), rolled[:SUBLANES])
        xd = jnp.concatenate([head, rolled[SUBLANES:]], axis=0)
        xc = xc + cw[CONV_WIDTH - 1 - d:CONV_WIDTH - d] * xd
    halo_ref[...] = x[tc - SUBLANES:]

    xcb = xc.astype(BF16)
    gates = [jnp.dot(xcb[:, n * LRU_BLOCK:(n + 1) * LRU_BLOCK], wg_ref[n],
                     preferred_element_type=F32) for n in range(width // LRU_BLOCK)]
    r = _sigmoid(jnp.concatenate([g[:, :LRU_BLOCK] for g in gates], axis=1) + ba_ref[...])
    i = _sigmoid(jnp.concatenate([g[:, LRU_BLOCK:] for g in gates], axis=1) + bx_ref[...])
    lam = lam_ref[...]
    log_sig_lam = jnp.minimum(lam, 0.0) - jnp.log(1.0 + jnp.exp(-jnp.abs(lam)))
    log_a = (LRU_C * log_sig_lam) * r
    a = jnp.exp(log_a)
    u = jnp.sqrt(1.0 - a * a) * (i * xc)

    rows = lax.broadcasted_iota(jnp.int32, (tc, width), 0)
    d = 1
    while d < tc:
        if d < SUBLANES:
            a_sh = jnp.where(rows >= d, pltpu.roll(a, d, axis=0), 1.0)
            u_sh = jnp.where(rows >= d, pltpu.roll(u, d, axis=0), 0.0)
            u = u + a * u_sh
            a = a * a_sh
        else:
            u_tail = u[d:] + a[d:] * u[:tc - d]
            a_tail = a[d:] * a[:tc - d]
            u = jnp.concatenate([u[:d], u_tail], axis=0)
            a = jnp.concatenate([a[:d], a_tail], axis=0)
        d *= 2
    h = u + a * h_ref[...]
    h_ref[...] = h[tc - 1:tc]
    y_ref[...] = (h * _silu(ag_ref[...].astype(F32))).astype(y_ref.dtype)


def _rglru(a_x, a_g, conv_w, conv_b, w_gates_bf16, b_a, b_x, lam, *, batch, tc, nb):
    t, width = a_x.shape
    seq = t // batch
    cw = nb * LRU_BLOCK
    nblk = width // cw
    nt = seq // tc
    assert seq % tc == 0 and width % cw == 0
    row_blk = lambda b, n, s: (b * nt + s, n)
    par_blk = lambda b, n, s: (0, n)
    return pl.pallas_call(
        functools.partial(_rglru_kernel, tc=tc),
        grid=(batch, nblk, nt),
        in_specs=[
            pl.BlockSpec((tc, cw), row_blk),
            pl.BlockSpec((tc, cw), row_blk),
            pl.BlockSpec((CONV_WIDTH, cw), par_blk),
            pl.BlockSpec((1, cw), par_blk),
            pl.BlockSpec((nb, LRU_BLOCK, 2 * LRU_BLOCK), lambda b, n, s: (n, 0, 0)),
            pl.BlockSpec((1, cw), par_blk),
            pl.BlockSpec((1, cw), par_blk),
            pl.BlockSpec((1, cw), par_blk),
        ],
        out_specs=pl.BlockSpec((tc, cw), row_blk),
        out_shape=jax.ShapeDtypeStruct((t, width), BF16),
        scratch_shapes=[pltpu.VMEM((SUBLANES, cw), F32), pltpu.VMEM((1, cw), F32)],
        compiler_params=pltpu.CompilerParams(
            dimension_semantics=("arbitrary", "arbitrary", "arbitrary"),
            vmem_limit_bytes=_vmem_limit(24 << 20)),
        name="rglru",
    )(a_x, a_g, conv_w, conv_b.reshape(1, width), w_gates_bf16, b_a.reshape(1, width),
      b_x.reshape(1, width), lam.reshape(1, width))


def _stickbreak_kernel(q_ref, k_ref, v_ref, g_ref, o_ref, ntri_ref, z_ref, w_ref, acc_ref,
                       *, tq, tk, nh):
    dh = SB_HEAD_DIM
    i = pl.program_id(2)
    nsub = tq // tk
    chains = [(s, h) for s in range(nsub) for h in range(nh)]

    @pl.when((pl.program_id(0) == 0) & (pl.program_id(1) == 0) & (i == 0))
    def _():
        r = lax.broadcasted_iota(jnp.int32, (tk, tk), 0)
        c = lax.broadcasted_iota(jnp.int32, (tk, tk), 1)
        ntri_ref[...] = jnp.where(r >= c, -1.0, 0.0).astype(BF16)

    ntri = ntri_ref[...]

    def key_block(g, s):
        return jnp.maximum((i - g) * nsub + (nsub - 1 - s), 0)

    def scores(g, c):
        s, h = chains[c]
        start = pl.multiple_of(key_block(g, s) * tk, tk)
        kb = k_ref[pl.ds(start, tk), h * dh:(h + 1) * dh]
        z_ref[c] = lax.dot_general(
            q_ref[:, h * dh:(h + 1) * dh], kb, (((1,), (1,)), ((), ())),
            preferred_element_type=F32)

    def weighted_values(g, c):
        s, h = chains[c]
        start = pl.multiple_of(key_block(g, s) * tk, tk)
        vb = v_ref[pl.ds(start, tk), h * dh:(h + 1) * dh]
        acc_ref[:, h * dh:(h + 1) * dh] += jnp.dot(w_ref[c], vb, preferred_element_type=F32)

    def causal_masks(g):
        rows = i * tq + lax.broadcasted_iota(jnp.int32, (tq, tk), 0)
        cols = lax.broadcasted_iota(jnp.int32, (tq, tk), 1)
        return [key_block(g, s) * tk + cols < rows for s in range(nsub)]

    def step(g, carries, causal=None):
        carries = list(carries)
        logits = []
        for c, (s, h) in enumerate(chains):
            z = z_ref[c]
            sp = jnp.maximum(z, 0.0) + jnp.log(1.0 + jnp.exp2(-jnp.abs(z))) * LOG2E
            if causal is not None:
                sp = jnp.where(causal[s], sp, 0.0)
            suffix = jnp.dot(sp.astype(BF16), ntri, preferred_element_type=F32)
            logits.append(z + suffix + carries[h])
            carries[h] = carries[h] - jnp.sum(sp, axis=-1, keepdims=True)
            if causal is None:
                weighted_values(g - 1, c)
            scores(g + 1, c)
        for c, (s, h) in enumerate(chains):
            w = jnp.exp2(logits[c])
            if causal is not None:
                w = jnp.where(causal[s], w, 0.0)
            w_ref[c] = w.astype(BF16)
        return tuple(carries)

    acc_ref[...] = jnp.zeros_like(acc_ref)
    for c in range(len(chains)):
        scores(0, c)
    carries = step(0, tuple(jnp.zeros((tq, 1), F32) for _ in range(nh)), causal_masks(0))
    lax.fori_loop(1, i + 1, step, carries)
    for c in range(len(chains)):
        weighted_values(i, c)
    o_ref[...] = (acc_ref[...] * _silu(g_ref[...].astype(F32))).astype(o_ref.dtype)


def _stickbreak(q, k, v, g, *, batch, tq, tk, nh):
    t, width = q.shape
    seq = t // batch
    gw = nh * SB_HEAD_DIM
    groups = width // gw
    nq = seq // tq
    assert seq % tq == 0 and tq % tk == 0 and width % gw == 0
    q_blk = lambda b, h, i: (b * nq + i, h)
    kv_blk = lambda b, h, i: (b, h)
    nchain = (tq // tk) * nh
    vmem = 2 * 2 * seq * gw * 2 + nchain * tq * tk * (4 + 2 + 8 * 4)
    return pl.pallas_call(
        functools.partial(_stickbreak_kernel, tq=tq, tk=tk, nh=nh),
        grid=(batch, groups, nq),
        in_specs=[
            pl.BlockSpec((tq, gw), q_blk),
            pl.BlockSpec((seq, gw), kv_blk),
            pl.BlockSpec((seq, gw), kv_blk),
            pl.BlockSpec((tq, gw), q_blk),
        ],
        out_specs=pl.BlockSpec((tq, gw), q_blk),
        out_shape=jax.ShapeDtypeStruct((t, width), BF16),
        scratch_shapes=[
            pltpu.VMEM((tk, tk), BF16),
            pltpu.VMEM((nchain, tq, tk), F32),
            pltpu.VMEM((nchain, tq, tk), BF16),
            pltpu.VMEM((tq, gw), F32),
        ],
        compiler_params=pltpu.CompilerParams(
            dimension_semantics=("arbitrary", "arbitrary", "arbitrary"),
            vmem_limit_bytes=_vmem_limit(vmem + (8 << 20))),
        name="stickbreak",
    )(q, k, v, g)


def _alibi_slope(h):
    return 2.0 ** (-8.0 * (h + 1) / C_HEADS)


def _swa_kernel(sink_ref, q_ref, kp_ref, kc_ref, vp_ref, vc_ref, g_ref, o_ref, *, tq, blocks_per_seq):
    nk = WINDOW + tq
    pairs = C_GROUP // 2
    first = (pl.program_id(0) % blocks_per_seq) == 0

    r = lax.broadcasted_iota(jnp.int32, (tq, nk), 0)
    c = lax.broadcasted_iota(jnp.int32, (tq, nk), 1)
    dist = r - c + WINDOW
    first_key = jnp.where(first, WINDOW, 0)
    valid = (dist >= 0) & (dist < WINDOW) & (c >= first_key)
    mdist = jnp.where(valid, dist.astype(F32), jnp.inf)

    lane = lax.broadcasted_iota(jnp.int32, (nk, LANES), 1)
    low = lane < C_HEAD_DIM

    def halves(prev_ref, cur_ref):
        full = jnp.concatenate([prev_ref[...], cur_ref[...]], axis=0).astype(F32)
        swapped = pltpu.roll(full, C_HEAD_DIM, axis=1)
        zero = jnp.zeros_like(full)
        even = [jnp.where(low, full, zero), jnp.where(low, swapped, zero)]
        odd = [jnp.where(low, zero, swapped), jnp.where(low, zero, full)]
        return [[e.astype(BF16) for e in even], [o.astype(BF16) for o in odd]]

    k_ext = halves(kp_ref, kc_ref)
    v_ext = halves(vp_ref, vc_ref)

    chains = [(kv, parity) for kv in range(C_KV_HEADS) for parity in range(2)]
    cols = [[(kv * pairs + p) * LANES for p in range(pairs)] for kv in range(C_KV_HEADS)]
    scores = []
    for kv, parity in chains:
        qs = jnp.concatenate([q_ref[:, c0:c0 + LANES] for c0 in cols[kv]], axis=0)
        scores.append(lax.dot_general(qs, k_ext[parity][kv], (((1,), (1,)), ((), ())),
                                      preferred_element_type=F32))
    outs = [None] * C_KV_HEADS
    for (kv, parity), s in zip(chains, scores):
        probs, invs = [], []
        for p in range(pairs):
            head = kv * C_GROUP + 2 * p + parity
            sink = sink_ref[head]
            sc = s[p * tq:(p + 1) * tq] - _alibi_slope(head) * mdist
            m = jnp.maximum(jnp.max(sc, axis=-1, keepdims=True), sink)
            pe = jnp.exp(sc - m)
            denom = jnp.sum(pe, axis=-1, keepdims=True) + jnp.exp(sink - m)
            probs.append(pe.astype(BF16))
            invs.append(1.0 / denom)
        o_par = jnp.dot(jnp.concatenate(probs, axis=0), v_ext[parity][kv],
                        preferred_element_type=F32)
        o_par = o_par * jnp.concatenate(invs, axis=0)
        outs[kv] = o_par if outs[kv] is None else outs[kv] + o_par
    for kv in range(C_KV_HEADS):
        for p, c0 in enumerate(cols[kv]):
            gate = _silu(g_ref[:, c0:c0 + LANES].astype(F32))
            o_ref[:, c0:c0 + LANES] = (outs[kv][p * tq:(p + 1) * tq] * gate).astype(o_ref.dtype)


def _swa(q, k, v, g, sinks, *, batch, tq):
    t, width = q.shape
    seq = t // batch
    assert seq % tq == 0 and tq % WINDOW == 0 and k.shape[1] == LANES
    blocks_per_seq = seq // tq
    ratio = tq // WINDOW
    cur = lambda i: (i, 0)
    prev = lambda i: (jnp.maximum(i * ratio - 1, 0), 0)
    return pl.pallas_call(
        functools.partial(_swa_kernel, tq=tq, blocks_per_seq=blocks_per_seq),
        grid=(t // tq,),
        in_specs=[
            pl.BlockSpec(memory_space=pltpu.SMEM),
            pl.BlockSpec((tq, width), cur),
            pl.BlockSpec((WINDOW, LANES), prev),
            pl.BlockSpec((tq, LANES), cur),
            pl.BlockSpec((WINDOW, LANES), prev),
            pl.BlockSpec((tq, LANES), cur),
            pl.BlockSpec((tq, width), cur),
        ],
        out_specs=pl.BlockSpec((tq, width), cur),
        out_shape=jax.ShapeDtypeStruct((t, width), BF16),
        compiler_params=pltpu.CompilerParams(
            dimension_semantics=("arbitrary",), vmem_limit_bytes=_vmem_limit(32 << 20)),
        name="swa",
    )(sinks, q, k, k, v, v, g)


def _even_layer(x2, batch, w_in, conv_w, conv_b, w_gate_a, b_gate_a, w_gate_x, b_gate_x, lam,
                w_out, ln_g, ln_b, alpha):
    d = x2.shape[1]
    a_x, a_g, b_q, b_k, b_v, b_g = _inproj(
        x2, w_in.astype(BF16), widths=[d] * 6, dtypes=[F32, BF16, BF16, BF16, BF16, BF16],
        scales=[1.0, 1.0, LOG2E / math.sqrt(SB_HEAD_DIM), 1.0, 1.0, 1.0], tm=512, name="inproj_even")
    w_gates = jnp.concatenate([w_gate_a, w_gate_x], axis=-1).astype(BF16)
    y_a = _rglru(a_x, a_g, conv_w, conv_b, w_gates, b_gate_a, b_gate_x, lam, batch=batch, tc=256, nb=4)
    y_b = _stickbreak(b_q, b_k, b_v, b_g, batch=batch, tq=512, tk=256, nh=2)
    return _outproj_ln([y_a, y_b], w_out.astype(BF16), x2, ln_g, ln_b, alpha, tm=512,
                       name="outproj_ln_even")


def _odd_layer(x2, batch, w_in, sinks, w_out, ln_g, ln_b, alpha):
    d = x2.shape[1]
    kv_width = C_KV_HEADS * C_HEAD_DIM
    q, k, v, g = _inproj(
        x2, w_in.astype(BF16), widths=[d, kv_width, kv_width, d], dtypes=[BF16] * 4,
        scales=[1.0 / math.sqrt(C_HEAD_DIM), 1.0, 1.0, 1.0], tm=512, name="inproj_odd")
    y = _swa(q, k, v, g, sinks, batch=batch, tq=WINDOW)
    return _outproj_ln([y], w_out.astype(BF16), x2, ln_g, ln_b, alpha, tm=512, name="outproj_ln_odd")


def kernel(x, e_w_in, e_conv_w, e_conv_b, e_w_gate_a, e_b_gate_a, e_w_gate_x, e_b_gate_x, e_lru_lambda, e_w_out, e_ln_g, e_ln_b, o_w_in, o_sinks, o_w_out, o_ln_g, o_ln_b):
    batch, seq, d = x.shape
    depth = e_w_in.shape[0] + o_w_in.shape[0]
    alpha = float((2 * depth) ** 0.25)
    x2 = x.reshape(batch * seq, d)
    for layer in range(depth):
        li = layer // 2
        if layer % 2 == 0:
            x2 = _even_layer(x2, batch, e_w_in[li], e_conv_w[li], e_conv_b[li], e_w_gate_a[li],
                             e_b_gate_a[li], e_w_gate_x[li], e_b_gate_x[li], e_lru_lambda[li],
                             e_w_out[li], e_ln_g[li], e_ln_b[li], alpha)
        else:
            x2 = _odd_layer(x2, batch, o_w_in[li], o_sinks[li], o_w_out[li], o_ln_g[li], o_ln_b[li],
                            alpha)
    return x2.reshape(batch, seq, d)
```

```python
import functools
import math

import jax
import jax.numpy as jnp
from jax import lax
from jax.experimental import pallas as pl
from jax.experimental.pallas import tpu as pltpu

F32 = jnp.float32
BF16 = jnp.bfloat16

LANES = 128
SUBLANES = 8
VMEM_BYTES_V7X = 64 * 1024 * 1024

LRU_BLOCK = 128
LRU_C = 8.0
CONV_WIDTH = 4
SB_HEAD_DIM = 128
C_HEADS = 16
C_KV_HEADS = 2
C_GROUP = C_HEADS // C_KV_HEADS
C_HEAD_DIM = 64
WINDOW = 128
LN_EPS = 1e-5
LOG2E = math.log2(math.e)


def _sigmoid(x):
    return 0.5 * jnp.tanh(0.5 * x) + 0.5


def _silu(x):
    return x * _sigmoid(x)


def _vmem_limit(nbytes):
    return int(min(nbytes, VMEM_BYTES_V7X - 8 * 1024 * 1024))


def _inproj_kernel(x_ref, w_ref, *out_refs, widths, scales):
    xb = x_ref[...].astype(BF16)
    col = 0
    for o_ref, width, scale in zip(out_refs, widths, scales):
        r = jnp.dot(xb, w_ref[:, col:col + width], preferred_element_type=F32)
        if scale != 1.0:
            r = r * scale
        o_ref[...] = r.astype(o_ref.dtype)
        col += width


def _inproj(x2, w_bf16, widths, dtypes, scales, *, tm, name):
    t, d = x2.shape
    n = w_bf16.shape[1]
    assert sum(widths) == n and t % tm == 0
    out_bytes = sum(tm * wd * jnp.dtype(dt).itemsize for wd, dt in zip(widths, dtypes))
    vmem = d * n * 2 + 2 * tm * d * 4 + 2 * out_bytes + tm * d * 2 + 3 * tm * max(widths) * 4
    return pl.pallas_call(
        functools.partial(_inproj_kernel, widths=tuple(widths), scales=tuple(scales)),
        grid=(t // tm,),
        in_specs=[
            pl.BlockSpec((tm, d), lambda i: (i, 0)),
            pl.BlockSpec((d, n), lambda i: (0, 0), pipeline_mode=pl.Buffered(1)),
        ],
        out_specs=[pl.BlockSpec((tm, wd), lambda i: (i, 0)) for wd in widths],
        out_shape=[jax.ShapeDtypeStruct((t, wd), dt) for wd, dt in zip(widths, dtypes)],
        compiler_params=pltpu.CompilerParams(
            dimension_semantics=("arbitrary",), vmem_limit_bytes=_vmem_limit(vmem + (8 << 20))),
        name=name,
    )(x2, w_bf16)


def _outproj_ln_kernel(*refs, n_y, alpha):
    y_refs = refs[:n_y]
    w_ref, x_ref, g_ref, b_ref, o_ref = refs[n_y:]
    acc = alpha * x_ref[...]
    row = 0
    for y_ref in y_refs:
        kdim = y_ref.shape[1]
        acc = acc + jnp.dot(y_ref[...], w_ref[row:row + kdim, :], preferred_element_type=F32)
        row += kdim
    mu = jnp.mean(acc, axis=-1, keepdims=True)
    dev = acc - mu
    var = jnp.mean(dev * dev, axis=-1, keepdims=True)
    o_ref[...] = dev * lax.rsqrt(var + LN_EPS) * g_ref[...] + b_ref[...]


def _outproj_ln(ys, w_bf16, x2, ln_g, ln_b, alpha, *, tm, name):
    t, d = x2.shape
    kdim = w_bf16.shape[0]
    assert sum(y.shape[1] for y in ys) == kdim and t % tm == 0
    vmem = kdim * d * 2 + 2 * tm * kdim * 2 + 4 * tm * d * 4 + 4 * tm * d * 4
    return pl.pallas_call(
        functools.partial(_outproj_ln_kernel, n_y=len(ys), alpha=alpha),
        grid=(t // tm,),
        in_specs=(
            [pl.BlockSpec((tm, y.shape[1]), lambda i: (i, 0)) for y in ys]
            + [
                pl.BlockSpec((kdim, d), lambda i: (0, 0), pipeline_mode=pl.Buffered(1)),
                pl.BlockSpec((tm, d), lambda i: (i, 0)),
                pl.BlockSpec((1, d), lambda i: (0, 0)),
                pl.BlockSpec((1, d), lambda i: (0, 0)),
            ]
        ),
        out_specs=pl.BlockSpec((tm, d), lambda i: (i, 0)),
        out_shape=jax.ShapeDtypeStruct((t, d), F32),
        compiler_params=pltpu.CompilerParams(
            dimension_semantics=("arbitrary",), vmem_limit_bytes=_vmem_limit(vmem + (8 << 20))),
        name=name,
    )(*ys, w_bf16, x2, ln_g.reshape(1, d), ln_b.reshape(1, d))


def _rglru_kernel(ax_ref, ag_ref, cw_ref, cb_ref, wg_ref, ba_ref, bx_ref, lam_ref, y_ref,
                  halo_ref, h_ref, *, tc):
    @pl.when(pl.program_id(2) == 0)
    def _():
        halo_ref[...] = jnp.zeros_like(halo_ref)
        h_ref[...] = jnp.zeros_like(h_ref)

    x = ax_ref[...]
    width = x.shape[1]
    halo = halo_ref[...]
    row8 = lax.broadcasted_iota(jnp.int32, (SUBLANES, width), 0)
    cw = cw_ref[...]
    xc = cb_ref[...] + cw[CONV_WIDTH - 1:CONV_WIDTH] * x
    for d in range(1, CONV_WIDTH):
        rolled = pltpu.roll(x, d, axis=0)
        head = jnp.where(row8 < d, pltpu.roll(halo, d, axis=0), rolled[:SUBLANES])
        xd = jnp.concatenate([head, rolled[SUBLANES:]], axis=0)
        xc = xc + cw[CONV_WIDTH - 1 - d:CONV_WIDTH - d] * xd
    halo_ref[...] = x[tc - SUBLANES:]

    xcb = xc.astype(BF16)
    gates = [jnp.dot(xcb[:, n * LRU_BLOCK:(n + 1) * LRU_BLOCK], wg_ref[n],
                     preferred_element_type=F32) for n in range(width // LRU_BLOCK)]
    r = _sigmoid(jnp.concatenate([g[:, :LRU_BLOCK] for g in gates], axis=1) + ba_ref[...])
    i = _sigmoid(jnp.concatenate([g[:, LRU_BLOCK:] for g in gates], axis=1) + bx_ref[...])
    lam = lam_ref[...]
    log_sig_lam = jnp.minimum(lam, 0.0) - jnp.log(1.0 + jnp.exp(-jnp.abs(lam)))
    log_a = (LRU_C * log_sig_lam) * r
    a = jnp.exp(log_a)
    u = jnp.sqrt(1.0 - a * a) * (i * xc)

    rows = lax.broadcasted_iota(jnp.int32, (tc, width), 0)
    d = 1
    while d < tc:
        if d < SUBLANES:
            a_sh = jnp.where(rows >= d, pltpu.roll(a, d, axis=0), 1.0)
            u_sh = jnp.where(rows >= d, pltpu.roll(u, d, axis=0), 0.0)
            u = u + a * u_sh
            a = a * a_sh
        else:
            u_tail = u[d:] + a[d:] * u[:tc - d]
            a_tail = a[d:] * a[:tc - d]
            u = jnp.concatenate([u[:d], u_tail], axis=0)
            a = jnp.concatenate([a[:d], a_tail], axis=0)
        d *= 2
    h = u + a * h_ref[...]
    h_ref[...] = h[tc - 1:tc]
    y_ref[...] = (h * _silu(ag_ref[...].astype(F32))).astype(y_ref.dtype)


def _rglru(a_x, a_g, conv_w, conv_b, w_gates_bf16, b_a, b_x, lam, *, batch, tc, nb):
    t, width = a_x.shape
    seq = t // batch
    cw = nb * LRU_BLOCK
    nblk = width // cw
    nt = seq // tc
    assert seq % tc == 0 and width % cw == 0
    row_blk = lambda b, n, s: (b * nt + s, n)
    par_blk = lambda b, n, s: (0, n)
    return pl.pallas_call(
        functools.partial(_rglru_kernel, tc=tc),
        grid=(batch, nblk, nt),
        in_specs=[
            pl.BlockSpec((tc, cw), row_blk),
            pl.BlockSpec((tc, cw), row_blk),
            pl.BlockSpec((CONV_WIDTH, cw), par_blk),
            pl.BlockSpec((1, cw), par_blk),
            pl.BlockSpec((nb, LRU_BLOCK, 2 * LRU_BLOCK), lambda b, n, s: (n, 0, 0)),
            pl.BlockSpec((1, cw), par_blk),
            pl.BlockSpec((1, cw), par_blk),
            pl.BlockSpec((1, cw), par_blk),
        ],
        out_specs=pl.BlockSpec((tc, cw), row_blk),
        out_shape=jax.ShapeDtypeStruct((t, width), BF16),
        scratch_shapes=[pltpu.VMEM((SUBLANES, cw), F32), pltpu.VMEM((1, cw), F32)],
        compiler_params=pltpu.CompilerParams(
            dimension_semantics=("arbitrary", "arbitrary", "arbitrary"),
            vmem_limit_bytes=_vmem_limit(24 << 20)),
        name="rglru",
    )(a_x, a_g, conv_w, conv_b.reshape(1, width), w_gates_bf16, b_a.reshape(1, width),
      b_x.reshape(1, width), lam.reshape(1, width))


def _stickbreak_kernel(q_ref, k_ref, v_ref, g_ref, o_ref, ntri_ref, z_ref, w_ref, acc_ref,
                       *, tq, tk, nh):
    dh = SB_HEAD_DIM
    i = pl.program_id(2)
    nsub = tq // tk
    chains = [(s, h) for s in range(nsub) for h in range(nh)]

    @pl.when((pl.program_id(0) == 0) & (pl.program_id(1) == 0) & (i == 0))
    def _():
        r = lax.broadcasted_iota(jnp.int32, (tk, tk), 0)
        c = lax.broadcasted_iota(jnp.int32, (tk, tk), 1)
        ntri_ref[...] = jnp.where(r >= c, -1.0, 0.0).astype(BF16)

    ntri = ntri_ref[...]

    def key_block(g, s):
        return jnp.maximum((i - g) * nsub + (nsub - 1 - s), 0)

    def scores(g, c):
        s, h = chains[c]
        start = pl.multiple_of(key_block(g, s) * tk, tk)
        kb = k_ref[pl.ds(start, tk), h * dh:(h + 1) * dh]
        z_ref[c] = lax.dot_general(
            q_ref[:, h * dh:(h + 1) * dh], kb, (((1,), (1,)), ((), ())),
            preferred_element_type=F32).astype(z_ref.dtype)

    def weighted_values(g, c):
        s, h = chains[c]
        start = pl.multiple_of(key_block(g, s) * tk, tk)
        vb = v_ref[pl.ds(start, tk), h * dh:(h + 1) * dh]
        acc_ref[:, h * dh:(h + 1) * dh] += jnp.dot(w_ref[c], vb, preferred_element_type=F32)

    def causal_masks(g):
        rows = i * tq + lax.broadcasted_iota(jnp.int32, (tq, tk), 0)
        cols = lax.broadcasted_iota(jnp.int32, (tq, tk), 1)
        return [key_block(g, s) * tk + cols < rows for s in range(nsub)]

    def step(g, carries, causal=None):
        carries = list(carries)
        logits = []
        for c, (s, h) in enumerate(chains):
            z = z_ref[c]
            sp = jnp.maximum(z, 0) + jnp.log(1 + jnp.exp2(-jnp.abs(z))) * jnp.asarray(LOG2E, BF16)
            if causal is not None:
                sp = jnp.where(causal[s], sp, jnp.zeros_like(sp))
            suffix = jnp.dot(sp, ntri, preferred_element_type=F32)
            logits.append(z.astype(F32) + suffix + carries[h])
            carries[h] = carries[h] - jnp.sum(sp.astype(F32), axis=-1, keepdims=True)
            if causal is None:
                weighted_values(g - 1, c)
            scores(g + 1, c)
        for c, (s, h) in enumerate(chains):
            w = jnp.exp2(logits[c])
            if causal is not None:
                w = jnp.where(causal[s], w, 0.0)
            w_ref[c] = w.astype(BF16)
        return tuple(carries)

    acc_ref[...] = jnp.zeros_like(acc_ref)
    for c in range(len(chains)):
        scores(0, c)
    carries = step(0, tuple(jnp.zeros((tq, 1), F32) for _ in range(nh)), causal_masks(0))
    lax.fori_loop(1, i + 1, step, carries)
    for c in range(len(chains)):
        weighted_values(i, c)
    o_ref[...] = (acc_ref[...] * _silu(g_ref[...].astype(F32))).astype(o_ref.dtype)


def _stickbreak(q, k, v, g, *, batch, tq, tk, nh):
    t, width = q.shape
    seq = t // batch
    gw = nh * SB_HEAD_DIM
    groups = width // gw
    nq = seq // tq
    assert seq % tq == 0 and tq % tk == 0 and width % gw == 0
    q_blk = lambda b, h, i: (b * nq + i, h)
    kv_blk = lambda b, h, i: (b, h)
    nchain = (tq // tk) * nh
    vmem = 2 * 2 * seq * gw * 2 + nchain * tq * tk * (2 + 2 + 8 * 4)
    return pl.pallas_call(
        functools.partial(_stickbreak_kernel, tq=tq, tk=tk, nh=nh),
        grid=(batch, groups, nq),
        in_specs=[
            pl.BlockSpec((tq, gw), q_blk),
            pl.BlockSpec((seq, gw), kv_blk),
            pl.BlockSpec((seq, gw), kv_blk),
            pl.BlockSpec((tq, gw), q_blk),
        ],
        out_specs=pl.BlockSpec((tq, gw), q_blk),
        out_shape=jax.ShapeDtypeStruct((t, width), BF16),
        scratch_shapes=[
            pltpu.VMEM((tk, tk), BF16),
            pltpu.VMEM((nchain, tq, tk), BF16),
            pltpu.VMEM((nchain, tq, tk), BF16),
            pltpu.VMEM((tq, gw), F32),
        ],
        compiler_params=pltpu.CompilerParams(
            dimension_semantics=("arbitrary", "arbitrary", "arbitrary"),
            vmem_limit_bytes=_vmem_limit(vmem + (8 << 20))),
        name="stickbreak",
    )(q, k, v, g)


def _alibi_slope(h):
    return 2.0 ** (-8.0 * (h + 1) / C_HEADS)


def _swa_kernel(sink_ref, q_ref, kp_ref, kc_ref, vp_ref, vc_ref, g_ref, o_ref, *, tq, blocks_per_seq):
    nk = WINDOW + tq
    pairs = C_GROUP // 2
    first = (pl.program_id(0) % blocks_per_seq) == 0

    r = lax.broadcasted_iota(jnp.int32, (tq, nk), 0)
    c = lax.broadcasted_iota(jnp.int32, (tq, nk), 1)
    dist = r - c + WINDOW
    first_key = jnp.where(first, WINDOW, 0)
    valid = (dist >= 0) & (dist < WINDOW) & (c >= first_key)
    mdist = jnp.where(valid, dist.astype(F32), jnp.inf)

    lane = lax.broadcasted_iota(jnp.int32, (nk, LANES), 1)
    low = lane < C_HEAD_DIM

    def halves(prev_ref, cur_ref):
        full = jnp.concatenate([prev_ref[...], cur_ref[...]], axis=0).astype(F32)
        swapped = pltpu.roll(full, C_HEAD_DIM, axis=1)
        zero = jnp.zeros_like(full)
        even = [jnp.where(low, full, zero), jnp.where(low, swapped, zero)]
        odd = [jnp.where(low, zero, swapped), jnp.where(low, zero, full)]
        return [[e.astype(BF16) for e in even], [o.astype(BF16) for o in odd]]

    k_ext = halves(kp_ref, kc_ref)
    v_ext = halves(vp_ref, vc_ref)

    chains = [(kv, parity) for kv in range(C_KV_HEADS) for parity in range(2)]
    cols = [[(kv * pairs + p) * LANES for p in range(pairs)] for kv in range(C_KV_HEADS)]
    scores = []
    for kv, parity in chains:
        qs = jnp.concatenate([q_ref[:, c0:c0 + LANES] for c0 in cols[kv]], axis=0)
        scores.append(lax.dot_general(qs, k_ext[parity][kv], (((1,), (1,)), ((), ())),
                                      preferred_element_type=F32))
    outs = [None] * C_KV_HEADS
    for (kv, parity), s in zip(chains, scores):
        probs, invs = [], []
        for p in range(pairs):
            head = kv * C_GROUP + 2 * p + parity
            sink = sink_ref[head]
            sc = s[p * tq:(p + 1) * tq] - _alibi_slope(head) * mdist
            m = jnp.maximum(jnp.max(sc, axis=-1, keepdims=True), sink)
            pe = jnp.exp(sc - m)
            denom = jnp.sum(pe, axis=-1, keepdims=True) + jnp.exp(sink - m)
            probs.append(pe.astype(BF16))
            invs.append(1.0 / denom)
        o_par = jnp.dot(jnp.concatenate(probs, axis=0), v_ext[parity][kv],
                        preferred_element_type=F32)
        o_par = o_par * jnp.concatenate(invs, axis=0)
        outs[kv] = o_par if outs[kv] is None else outs[kv] + o_par
    for kv in range(C_KV_HEADS):
        for p, c0 in enumerate(cols[kv]):
            gate = _silu(g_ref[:, c0:c0 + LANES].astype(F32))
            o_ref[:, c0:c0 + LANES] = (outs[kv][p * tq:(p + 1) * tq] * gate).astype(o_ref.dtype)


def _swa(q, k, v, g, sinks, *, batch, tq):
    t, width = q.shape
    seq = t // batch
    assert seq % tq == 0 and tq % WINDOW == 0 and k.shape[1] == LANES
    blocks_per_seq = seq // tq
    ratio = tq // WINDOW
    cur = lambda i: (i, 0)
    prev = lambda i: (jnp.maximum(i * ratio - 1, 0), 0)
    return pl.pallas_call(
        functools.partial(_swa_kernel, tq=tq, blocks_per_seq=blocks_per_seq),
        grid=(t // tq,),
        in_specs=[
            pl.BlockSpec(memory_space=pltpu.SMEM),
            pl.BlockSpec((tq, width), cur),
            pl.BlockSpec((WINDOW, LANES), prev),
            pl.BlockSpec((tq, LANES), cur),
            pl.BlockSpec((WINDOW, LANES), prev),
            pl.BlockSpec((tq, LANES), cur),
            pl.BlockSpec((tq, width), cur),
        ],
        out_specs=pl.BlockSpec((tq, width), cur),
        out_shape=jax.ShapeDtypeStruct((t, width), BF16),
        compiler_params=pltpu.CompilerParams(
            dimension_semantics=("arbitrary",), vmem_limit_bytes=_vmem_limit(32 << 20)),
        name="swa",
    )(sinks, q, k, k, v, v, g)


def _even_layer(x2, batch, w_in, conv_w, conv_b, w_gate_a, b_gate_a, w_gate_x, b_gate_x, lam,
                w_out, ln_g, ln_b, alpha):
    d = x2.shape[1]
    a_x, a_g, b_q, b_k, b_v, b_g = _inproj(
        x2, w_in.astype(BF16), widths=[d] * 6, dtypes=[F32, BF16, BF16, BF16, BF16, BF16],
        scales=[1.0, 1.0, LOG2E / math.sqrt(SB_HEAD_DIM), 1.0, 1.0, 1.0], tm=512, name="inproj_even")
    w_gates = jnp.concatenate([w_gate_a, w_gate_x], axis=-1).astype(BF16)
    y_a = _rglru(a_x, a_g, conv_w, conv_b, w_gates, b_gate_a, b_gate_x, lam, batch=batch, tc=256, nb=4)
    y_b = _stickbreak(b_q, b_k, b_v, b_g, batch=batch, tq=512, tk=256, nh=2)
    return _outproj_ln([y_a, y_b], w_out.astype(BF16), x2, ln_g, ln_b, alpha, tm=512,
                       name="outproj_ln_even")


def _odd_layer(x2, batch, w_in, sinks, w_out, ln_g, ln_b, alpha):
    d = x2.shape[1]
    kv_width = C_KV_HEADS * C_HEAD_DIM
    q, k, v, g = _inproj(
        x2, w_in.astype(BF16), widths=[d, kv_width, kv_width, d], dtypes=[BF16] * 4,
        scales=[1.0 / math.sqrt(C_HEAD_DIM), 1.0, 1.0, 1.0], tm=512, name="inproj_odd")
    y = _swa(q, k, v, g, sinks, batch=batch, tq=WINDOW)
    return _outproj_ln([y], w_out.astype(BF16), x2, ln_g, ln_b, alpha, tm=512, name="outproj_ln_odd")


def kernel(x, e_w_in, e_conv_w, e_conv_b, e_w_gate_a, e_b_gate_a, e_w_gate_x, e_b_gate_x, e_lru_lambda, e_w_out, e_ln_g, e_ln_b, o_w_in, o_sinks, o_w_out, o_ln_g, o_ln_b):
    batch, seq, d = x.shape
    depth = e_w_in.shape[0] + o_w_in.shape[0]
    alpha = float((2 * depth) ** 0.25)
    x2 = x.reshape(batch * seq, d)
    for layer in range(depth):
        li = layer // 2
        if layer % 2 == 0:
            x2 = _even_layer(x2, batch, e_w_in[li], e_conv_w[li], e_conv_b[li], e_w_gate_a[li],
                             e_b_gate_a[li], e_w_gate_x[li], e_b_gate_x[li], e_lru_lambda[li],
                             e_w_out[li], e_ln_g[li], e_ln_b[li], alpha)
        else:
            x2 = _odd_layer(x2, batch, o_w_in[li], o_sinks[li], o_w_out[li], o_ln_g[li], o_ln_b[li],
                            alpha)
    return x2.reshape(batch, seq, d)
```

```python
import functools
import math

import jax
import jax.numpy as jnp
from jax import lax
from jax.experimental import pallas as pl
from jax.experimental.pallas import tpu as pltpu

F32 = jnp.float32
BF16 = jnp.bfloat16

LANES = 128
SUBLANES = 8
VMEM_BYTES_V7X = 64 * 1024 * 1024

LRU_BLOCK = 128
LRU_C = 8.0
CONV_WIDTH = 4
SB_HEAD_DIM = 128
C_HEADS = 16
C_KV_HEADS = 2
C_GROUP = C_HEADS // C_KV_HEADS
C_HEAD_DIM = 64
WINDOW = 128
LN_EPS = 1e-5


def _sigmoid(x):
    return 0.5 * jnp.tanh(0.5 * x) + 0.5


def _silu(x):
    return x * _sigmoid(x)


def _vmem_limit(nbytes):
    return int(min(nbytes, VMEM_BYTES_V7X - 8 * 1024 * 1024))


def _inproj_kernel(x_ref, w_ref, *out_refs, widths, scales):
    xb = x_ref[...].astype(BF16)
    col = 0
    for o_ref, width, scale in zip(out_refs, widths, scales):
        r = jnp.dot(xb, w_ref[:, col:col + width], preferred_element_type=F32)
        if scale != 1.0:
            r = r * scale
        o_ref[...] = r.astype(o_ref.dtype)
        col += width


def _inproj(x2, w_bf16, widths, dtypes, scales, *, tm, name):
    t, d = x2.shape
    n = w_bf16.shape[1]
    assert sum(widths) == n and t % tm == 0
    out_bytes = sum(tm * wd * jnp.dtype(dt).itemsize for wd, dt in zip(widths, dtypes))
    vmem = d * n * 2 + 2 * tm * d * 4 + 2 * out_bytes + tm * d * 2 + 3 * tm * max(widths) * 4
    return pl.pallas_call(
        functools.partial(_inproj_kernel, widths=tuple(widths), scales=tuple(scales)),
        grid=(t // tm,),
        in_specs=[
            pl.BlockSpec((tm, d), lambda i: (i, 0)),
            pl.BlockSpec((d, n), lambda i: (0, 0), pipeline_mode=pl.Buffered(1)),
        ],
        out_specs=[pl.BlockSpec((tm, wd), lambda i: (i, 0)) for wd in widths],
        out_shape=[jax.ShapeDtypeStruct((t, wd), dt) for wd, dt in zip(widths, dtypes)],
        compiler_params=pltpu.CompilerParams(
            dimension_semantics=("arbitrary",), vmem_limit_bytes=_vmem_limit(vmem + (8 << 20))),
        name=name,
    )(x2, w_bf16)


def _outproj_ln_kernel(*refs, n_y, alpha):
    y_refs = refs[:n_y]
    w_ref, x_ref, g_ref, b_ref, o_ref = refs[n_y:]
    acc = alpha * x_ref[...]
    row = 0
    for y_ref in y_refs:
        kdim = y_ref.shape[1]
        acc = acc + jnp.dot(y_ref[...], w_ref[row:row + kdim, :], preferred_element_type=F32)
        row += kdim
    mu = jnp.mean(acc, axis=-1, keepdims=True)
    dev = acc - mu
    var = jnp.mean(dev * dev, axis=-1, keepdims=True)
    o_ref[...] = dev * lax.rsqrt(var + LN_EPS) * g_ref[...] + b_ref[...]


def _outproj_ln(ys, w_bf16, x2, ln_g, ln_b, alpha, *, tm, name):
    t, d = x2.shape
    kdim = w_bf16.shape[0]
    assert sum(y.shape[1] for y in ys) == kdim and t % tm == 0
    vmem = kdim * d * 2 + 2 * tm * kdim * 2 + 4 * tm * d * 4 + 4 * tm * d * 4
    return pl.pallas_call(
        functools.partial(_outproj_ln_kernel, n_y=len(ys), alpha=alpha),
        grid=(t // tm,),
        in_specs=(
            [pl.BlockSpec((tm, y.shape[1]), lambda i: (i, 0)) for y in ys]
            + [
                pl.BlockSpec((kdim, d), lambda i: (0, 0), pipeline_mode=pl.Buffered(1)),
                pl.BlockSpec((tm, d), lambda i: (i, 0)),
                pl.BlockSpec((1, d), lambda i: (0, 0)),
                pl.BlockSpec((1, d), lambda i: (0, 0)),
            ]
        ),
        out_specs=pl.BlockSpec((tm, d), lambda i: (i, 0)),
        out_shape=jax.ShapeDtypeStruct((t, d), F32),
        compiler_params=pltpu.CompilerParams(
            dimension_semantics=("arbitrary",), vmem_limit_bytes=_vmem_limit(vmem + (8 << 20))),
        name=name,
    )(*ys, w_bf16, x2, ln_g.reshape(1, d), ln_b.reshape(1, d))


def _rglru_kernel(ax_ref, ag_ref, cw_ref, cb_ref, wg_ref, ba_ref, bx_ref, lam_ref, y_ref,
                  halo_ref, h_ref, *, tc):
    @pl.when(pl.program_id(2) == 0)
    def _():
        halo_ref[...] = jnp.zeros_like(halo_ref)
        h_ref[...] = jnp.zeros_like(h_ref)

    x = ax_ref[...]
    width = x.shape[1]
    halo = halo_ref[...]
    row8 = lax.broadcasted_iota(jnp.int32, (SUBLANES, width), 0)
    cw = cw_ref[...]
    xc = cb_ref[...] + cw[CONV_WIDTH - 1:CONV_WIDTH] * x
    for d in range(1, CONV_WIDTH):
        rolled = pltpu.roll(x, d, axis=0)
        head = jnp.where(row8 < d, pltpu.roll(halo, d, axis=0), rolled[:SUBLANES])
        xd = jnp.concatenate([head, rolled[SUBLANES:]], axis=0)
        xc = xc + cw[CONV_WIDTH - 1 - d:CONV_WIDTH - d] * xd
    halo_ref[...] = x[tc - SUBLANES:]

    xcb = xc.astype(BF16)
    gates = [jnp.dot(xcb[:, n * LRU_BLOCK:(n + 1) * LRU_BLOCK], wg_ref[n],
                     preferred_element_type=F32) for n in range(width // LRU_BLOCK)]
    r = _sigmoid(jnp.concatenate([g[:, :LRU_BLOCK] for g in gates], axis=1) + ba_ref[...])
    i = _sigmoid(jnp.concatenate([g[:, LRU_BLOCK:] for g in gates], axis=1) + bx_ref[...])
    lam = lam_ref[...]
    log_sig_lam = jnp.minimum(lam, 0.0) - jnp.log(1.0 + jnp.exp(-jnp.abs(lam)))
    log_a = (LRU_C * log_sig_lam) * r
    a = jnp.exp(log_a)
    u = jnp.sqrt(1.0 - a * a) * (i * xc)

    rows = lax.broadcasted_iota(jnp.int32, (tc, width), 0)
    d = 1
    while d < tc:
        if d < SUBLANES:
            a_sh = jnp.where(rows >= d, pltpu.roll(a, d, axis=0), 1.0)
            u_sh = jnp.where(rows >= d, pltpu.roll(u, d, axis=0), 0.0)
            u = u + a * u_sh
            a = a * a_sh
        else:
            u_tail = u[d:] + a[d:] * u[:tc - d]
            a_tail = a[d:] * a[:tc - d]
            u = jnp.concatenate([u[:d], u_tail], axis=0)
            a = jnp.concatenate([a[:d], a_tail], axis=0)
        d *= 2
    h = u + a * h_ref[...]
    h_ref[...] = h[tc - 1:tc]
    y_ref[...] = (h * _silu(ag_ref[...].astype(F32))).astype(y_ref.dtype)


def _rglru(a_x, a_g, conv_w, conv_b, w_gates_bf16, b_a, b_x, lam, *, batch, tc, nb):
    t, width = a_x.shape
    seq = t // batch
    cw = nb * LRU_BLOCK
    nblk = width // cw
    nt = seq // tc
    assert seq % tc == 0 and width % cw == 0
    row_blk = lambda b, n, s: (b * nt + s, n)
    par_blk = lambda b, n, s: (0, n)
    return pl.pallas_call(
        functools.partial(_rglru_kernel, tc=tc),
        grid=(batch, nblk, nt),
        in_specs=[
            pl.BlockSpec((tc, cw), row_blk),
            pl.BlockSpec((tc, cw), row_blk),
            pl.BlockSpec((CONV_WIDTH, cw), par_blk),
            pl.BlockSpec((1, cw), par_blk),
            pl.BlockSpec((nb, LRU_BLOCK, 2 * LRU_BLOCK), lambda b, n, s: (n, 0, 0)),
            pl.BlockSpec((1, cw), par_blk),
            pl.BlockSpec((1, cw), par_blk),
            pl.BlockSpec((1, cw), par_blk),
        ],
        out_specs=pl.BlockSpec((tc, cw), row_blk),
        out_shape=jax.ShapeDtypeStruct((t, width), BF16),
        scratch_shapes=[pltpu.VMEM((SUBLANES, cw), F32), pltpu.VMEM((1, cw), F32)],
        compiler_params=pltpu.CompilerParams(
            dimension_semantics=("arbitrary", "arbitrary", "arbitrary"),
            vmem_limit_bytes=_vmem_limit(24 << 20)),
        name="rglru",
    )(a_x, a_g, conv_w, conv_b.reshape(1, width), w_gates_bf16, b_a.reshape(1, width),
      b_x.reshape(1, width), lam.reshape(1, width))


def _stickbreak_kernel(q_ref, k_ref, v_ref, g_ref, o_ref, ntri_ref, z_ref, w_ref, acc_ref,
                       *, tq, tk, nh):
    dh = SB_HEAD_DIM
    i = pl.program_id(2)
    nsub = tq // tk
    chains = [(s, h) for s in range(nsub) for h in range(nh)]

    @pl.when((pl.program_id(0) == 0) & (pl.program_id(1) == 0) & (i == 0))
    def _():
        r = lax.broadcasted_iota(jnp.int32, (tk, tk), 0)
        c = lax.broadcasted_iota(jnp.int32, (tk, tk), 1)
        ntri_ref[...] = jnp.where(r >= c, -1.0, 0.0).astype(BF16)

    ntri = ntri_ref[...]
    lane0 = lax.broadcasted_iota(jnp.int32, (tq, LANES), 1) == 0

    def key_block(g, s):
        return jnp.maximum((i - g) * nsub + (nsub - 1 - s), 0)

    def scores(g, c):
        s, h = chains[c]
        start = pl.multiple_of(key_block(g, s) * tk, tk)
        kb = k_ref[pl.ds(start, tk), h * dh:(h + 1) * dh]
        z_ref[c] = lax.dot_general(
            q_ref[:, h * dh:(h + 1) * dh], kb, (((1,), (1,)), ((), ())),
            preferred_element_type=F32).astype(z_ref.dtype)

    def weighted_values(g, c):
        s, h = chains[c]
        start = pl.multiple_of(key_block(g, s) * tk, tk)
        vb = v_ref[pl.ds(start, tk), h * dh:(h + 1) * dh]
        acc_ref[:, h * dh:(h + 1) * dh] += jnp.dot(w_ref[c], vb, preferred_element_type=F32)

    def causal_masks(g):
        rows = i * tq + lax.broadcasted_iota(jnp.int32, (tq, tk), 0)
        cols = lax.broadcasted_iota(jnp.int32, (tq, tk), 1)
        return [key_block(g, s) * tk + cols < rows for s in range(nsub)]

    def step(g, carries, causal=None):
        carries = list(carries)
        logits = []
        for c, (s, h) in enumerate(chains):
            z = z_ref[c]
            sp = jnp.maximum(z, 0) + jnp.log(1 + jnp.exp(-jnp.abs(z)))
            if causal is not None:
                sp = jnp.where(causal[s], sp, jnp.zeros_like(sp))
            suffix = jnp.dot(sp, ntri, preferred_element_type=F32)
            logits.append(z.astype(F32) + suffix + carries[h])
            first = jnp.where(lane0, suffix[:, :LANES], 0.0)
            carries[h] = carries[h] + jnp.sum(first, axis=-1, keepdims=True)
            if causal is None:
                weighted_values(g - 1, c)
            scores(g + 1, c)
        for c, (s, h) in enumerate(chains):
            w = jnp.exp(logits[c])
            if causal is not None:
                w = jnp.where(causal[s], w, 0.0)
            w_ref[c] = w.astype(BF16)
        return tuple(carries)

    acc_ref[...] = jnp.zeros_like(acc_ref)
    for c in range(len(chains)):
        scores(0, c)
    carries = step(0, tuple(jnp.zeros((tq, 1), F32) for _ in range(nh)), causal_masks(0))
    lax.fori_loop(1, i + 1, step, carries)
    for c in range(len(chains)):
        weighted_values(i, c)
    o_ref[...] = (acc_ref[...] * _silu(g_ref[...].astype(F32))).astype(o_ref.dtype)


def _stickbreak(q, k, v, g, *, batch, tq, tk, nh):
    t, width = q.shape
    seq = t // batch
    gw = nh * SB_HEAD_DIM
    groups = width // gw
    nq = seq // tq
    assert seq % tq == 0 and tq % tk == 0 and width % gw == 0
    q_blk = lambda b, h, i: (b * nq + i, h)
    kv_blk = lambda b, h, i: (b, h)
    nchain = (tq // tk) * nh
    vmem = 2 * 2 * seq * gw * 2 + nchain * tq * tk * (2 + 2 + 8 * 4)
    return pl.pallas_call(
        functools.partial(_stickbreak_kernel, tq=tq, tk=tk, nh=nh),
        grid=(batch, groups, nq),
        in_specs=[
            pl.BlockSpec((tq, gw), q_blk),
            pl.BlockSpec((seq, gw), kv_blk),
            pl.BlockSpec((seq, gw), kv_blk),
            pl.BlockSpec((tq, gw), q_blk),
        ],
        out_specs=pl.BlockSpec((tq, gw), q_blk),
        out_shape=jax.ShapeDtypeStruct((t, width), BF16),
        scratch_shapes=[
            pltpu.VMEM((tk, tk), BF16),
            pltpu.VMEM((nchain, tq, tk), BF16),
            pltpu.VMEM((nchain, tq, tk), BF16),
            pltpu.VMEM((tq, gw), F32),
        ],
        compiler_params=pltpu.CompilerParams(
            dimension_semantics=("arbitrary", "arbitrary", "arbitrary"),
            vmem_limit_bytes=_vmem_limit(vmem + (8 << 20))),
        name="stickbreak",
    )(q, k, v, g)


def _alibi_slope(h):
    return 2.0 ** (-8.0 * (h + 1) / C_HEADS)


def _swa_kernel(sink_ref, q_ref, kp_ref, kc_ref, vp_ref, vc_ref, g_ref, o_ref, *, tq, blocks_per_seq):
    nk = WINDOW + tq
    pairs = C_GROUP // 2
    first = (pl.program_id(0) % blocks_per_seq) == 0

    r = lax.broadcasted_iota(jnp.int32, (tq, nk), 0)
    c = lax.broadcasted_iota(jnp.int32, (tq, nk), 1)
    dist = r - c + WINDOW
    first_key = jnp.where(first, WINDOW, 0)
    valid = (dist >= 0) & (dist < WINDOW) & (c >= first_key)
    mdist = jnp.where(valid, dist.astype(F32), jnp.inf)

    lane = lax.broadcasted_iota(jnp.int32, (nk, LANES), 1)
    low = lane < C_HEAD_DIM

    def halves(prev_ref, cur_ref):
        full = jnp.concatenate([prev_ref[...], cur_ref[...]], axis=0).astype(F32)
        swapped = pltpu.roll(full, C_HEAD_DIM, axis=1)
        zero = jnp.zeros_like(full)
        even = [jnp.where(low, full, zero), jnp.where(low, swapped, zero)]
        odd = [jnp.where(low, zero, swapped), jnp.where(low, zero, full)]
        return [[e.astype(BF16) for e in even], [o.astype(BF16) for o in odd]]

    k_ext = halves(kp_ref, kc_ref)
    v_ext = halves(vp_ref, vc_ref)

    chains = [(kv, parity) for kv in range(C_KV_HEADS) for parity in range(2)]
    cols = [[(kv * pairs + p) * LANES for p in range(pairs)] for kv in range(C_KV_HEADS)]
    scores = []
    for kv, parity in chains:
        qs = jnp.concatenate([q_ref[:, c0:c0 + LANES] for c0 in cols[kv]], axis=0)
        scores.append(lax.dot_general(qs, k_ext[parity][kv], (((1,), (1,)), ((), ())),
                                      preferred_element_type=F32))
    outs = [None] * C_KV_HEADS
    for (kv, parity), s in zip(chains, scores):
        probs, invs = [], []
        for p in range(pairs):
            head = kv * C_GROUP + 2 * p + parity
            sink = sink_ref[head]
            sc = s[p * tq:(p + 1) * tq] - _alibi_slope(head) * mdist
            m = jnp.maximum(jnp.max(sc, axis=-1, keepdims=True), sink)
            pe = jnp.exp(sc - m)
            denom = jnp.sum(pe, axis=-1, keepdims=True) + jnp.exp(sink - m)
            probs.append(pe.astype(BF16))
            invs.append(1.0 / denom)
        o_par = jnp.dot(jnp.concatenate(probs, axis=0), v_ext[parity][kv],
                        preferred_element_type=F32)
        o_par = o_par * jnp.concatenate(invs, axis=0)
        outs[kv] = o_par if outs[kv] is None else outs[kv] + o_par
    for kv in range(C_KV_HEADS):
        for p, c0 in enumerate(cols[kv]):
            gate = _silu(g_ref[:, c0:c0 + LANES].astype(F32))
            o_ref[:, c0:c0 + LANES] = (outs[kv][p * tq:(p + 1) * tq] * gate).astype(o_ref.dtype)


def _swa(q, k, v, g, sinks, *, batch, tq):
    t, width = q.shape
    seq = t // batch
    assert seq % tq == 0 and tq % WINDOW == 0 and k.shape[1] == LANES
    blocks_per_seq = seq // tq
    ratio = tq // WINDOW
    cur = lambda i: (i, 0)
    prev = lambda i: (jnp.maximum(i * ratio - 1, 0), 0)
    return pl.pallas_call(
        functools.partial(_swa_kernel, tq=tq, blocks_per_seq=blocks_per_seq),
        grid=(t // tq,),
        in_specs=[
            pl.BlockSpec(memory_space=pltpu.SMEM),
            pl.BlockSpec((tq, width), cur),
            pl.BlockSpec((WINDOW, LANES), prev),
            pl.BlockSpec((tq, LANES), cur),
            pl.BlockSpec((WINDOW, LANES), prev),
            pl.BlockSpec((tq, LANES), cur),
            pl.BlockSpec((tq, width), cur),
        ],
        out_specs=pl.BlockSpec((tq, width), cur),
        out_shape=jax.ShapeDtypeStruct((t, width), BF16),
        compiler_params=pltpu.CompilerParams(
            dimension_semantics=("arbitrary",), vmem_limit_bytes=_vmem_limit(32 << 20)),
        name="swa",
    )(sinks, q, k, k, v, v, g)


def _even_layer(x2, batch, w_in, conv_w, conv_b, w_gate_a, b_gate_a, w_gate_x, b_gate_x, lam,
                w_out, ln_g, ln_b, alpha):
    d = x2.shape[1]
    a_x, a_g, b_q, b_k, b_v, b_g = _inproj(
        x2, w_in.astype(BF16), widths=[d] * 6, dtypes=[F32, BF16, BF16, BF16, BF16, BF16],
        scales=[1.0, 1.0, 1.0 / math.sqrt(SB_HEAD_DIM), 1.0, 1.0, 1.0], tm=512, name="inproj_even")
    w_gates = jnp.concatenate([w_gate_a, w_gate_x], axis=-1).astype(BF16)
    y_a = _rglru(a_x, a_g, conv_w, conv_b, w_gates, b_gate_a, b_gate_x, lam, batch=batch, tc=256, nb=4)
    y_b = _stickbreak(b_q, b_k, b_v, b_g, batch=batch, tq=512, tk=256, nh=2)
    return _outproj_ln([y_a, y_b], w_out.astype(BF16), x2, ln_g, ln_b, alpha, tm=512,
                       name="outproj_ln_even")


def _odd_layer(x2, batch, w_in, sinks, w_out, ln_g, ln_b, alpha):
    d = x2.shape[1]
    kv_width = C_KV_HEADS * C_HEAD_DIM
    q, k, v, g = _inproj(
        x2, w_in.astype(BF16), widths=[d, kv_width, kv_width, d], dtypes=[BF16] * 4,
        scales=[1.0 / math.sqrt(C_HEAD_DIM), 1.0, 1.0, 1.0], tm=512, name="inproj_odd")
    y = _swa(q, k, v, g, sinks, batch=batch, tq=WINDOW)
    return _outproj_ln([y], w_out.astype(BF16), x2, ln_g, ln_b, alpha, tm=512, name="outproj_ln_odd")


def kernel(x, e_w_in, e_conv_w, e_conv_b, e_w_gate_a, e_b_gate_a, e_w_gate_x, e_b_gate_x, e_lru_lambda, e_w_out, e_ln_g, e_ln_b, o_w_in, o_sinks, o_w_out, o_ln_g, o_ln_b):
    batch, seq, d = x.shape
    depth = e_w_in.shape[0] + o_w_in.shape[0]
    alpha = float((2 * depth) ** 0.25)
    x2 = x.reshape(batch * seq, d)
    for layer in range(depth):
        li = layer // 2
        if layer % 2 == 0:
            x2 = _even_layer(x2, batch, e_w_in[li], e_conv_w[li], e_conv_b[li], e_w_gate_a[li],
                             e_b_gate_a[li], e_w_gate_x[li], e_b_gate_x[li], e_lru_lambda[li],
                             e_w_out[li], e_ln_g[li], e_ln_b[li], alpha)
        else:
            x2 = _odd_layer(x2, batch, o_w_in[li], o_sinks[li], o_w_out[li], o_ln_g[li], o_ln_b[li],
                            alpha)
    return x2.reshape(batch, seq, d)
```

```python
import functools
import math

import jax
import jax.numpy as jnp
from jax import lax
from jax.experimental import pallas as pl
from jax.experimental.pallas import tpu as pltpu

F32 = jnp.float32
BF16 = jnp.bfloat16

LANES = 128
SUBLANES = 8
VMEM_BYTES_V7X = 64 * 1024 * 1024

LRU_BLOCK = 128
LRU_C = 8.0
CONV_WIDTH = 4
SB_HEAD_DIM = 128
C_HEADS = 16
C_KV_HEADS = 2
C_GROUP = C_HEADS // C_KV_HEADS
C_HEAD_DIM = 64
WINDOW = 128
LN_EPS = 1e-5
STICK_DEAD = -100.0


def _sigmoid(x):
    return 0.5 * jnp.tanh(0.5 * x) + 0.5


def _silu(x):
    return x * _sigmoid(x)


def _vmem_limit(nbytes):
    return int(min(nbytes, VMEM_BYTES_V7X - 8 * 1024 * 1024))


def _inproj_kernel(x_ref, w_ref, *out_refs, widths, scales):
    xb = x_ref[...].astype(BF16)
    col = 0
    for o_ref, width, scale in zip(out_refs, widths, scales):
        r = jnp.dot(xb, w_ref[:, col:col + width], preferred_element_type=F32)
        if scale != 1.0:
            r = r * scale
        o_ref[...] = r.astype(o_ref.dtype)
        col += width


def _inproj(x2, w_bf16, widths, dtypes, scales, *, tm, name):
    t, d = x2.shape
    n = w_bf16.shape[1]
    assert sum(widths) == n and t % tm == 0
    out_bytes = sum(tm * wd * jnp.dtype(dt).itemsize for wd, dt in zip(widths, dtypes))
    vmem = d * n * 2 + 2 * tm * d * 4 + 2 * out_bytes + tm * d * 2 + 3 * tm * max(widths) * 4
    return pl.pallas_call(
        functools.partial(_inproj_kernel, widths=tuple(widths), scales=tuple(scales)),
        grid=(t // tm,),
        in_specs=[
            pl.BlockSpec((tm, d), lambda i: (i, 0)),
            pl.BlockSpec((d, n), lambda i: (0, 0), pipeline_mode=pl.Buffered(1)),
        ],
        out_specs=[pl.BlockSpec((tm, wd), lambda i: (i, 0)) for wd in widths],
        out_shape=[jax.ShapeDtypeStruct((t, wd), dt) for wd, dt in zip(widths, dtypes)],
        compiler_params=pltpu.CompilerParams(
            dimension_semantics=("arbitrary",), vmem_limit_bytes=_vmem_limit(vmem + (8 << 20))),
        name=name,
    )(x2, w_bf16)


def _outproj_ln_kernel(*refs, n_y, alpha):
    y_refs = refs[:n_y]
    w_ref, x_ref, g_ref, b_ref, o_ref = refs[n_y:]
    acc = alpha * x_ref[...]
    row = 0
    for y_ref in y_refs:
        kdim = y_ref.shape[1]
        acc = acc + jnp.dot(y_ref[...], w_ref[row:row + kdim, :], preferred_element_type=F32)
        row += kdim
    mu = jnp.mean(acc, axis=-1, keepdims=True)
    dev = acc - mu
    var = jnp.mean(dev * dev, axis=-1, keepdims=True)
    o_ref[...] = dev * lax.rsqrt(var + LN_EPS) * g_ref[...] + b_ref[...]


def _outproj_ln(ys, w_bf16, x2, ln_g, ln_b, alpha, *, tm, name):
    t, d = x2.shape
    kdim = w_bf16.shape[0]
    assert sum(y.shape[1] for y in ys) == kdim and t % tm == 0
    vmem = kdim * d * 2 + 2 * tm * kdim * 2 + 4 * tm * d * 4 + 4 * tm * d * 4
    return pl.pallas_call(
        functools.partial(_outproj_ln_kernel, n_y=len(ys), alpha=alpha),
        grid=(t // tm,),
        in_specs=(
            [pl.BlockSpec((tm, y.shape[1]), lambda i: (i, 0)) for y in ys]
            + [
                pl.BlockSpec((kdim, d), lambda i: (0, 0), pipeline_mode=pl.Buffered(1)),
                pl.BlockSpec((tm, d), lambda i: (i, 0)),
                pl.BlockSpec((1, d), lambda i: (0, 0)),
                pl.BlockSpec((1, d), lambda i: (0, 0)),
            ]
        ),
        out_specs=pl.BlockSpec((tm, d), lambda i: (i, 0)),
        out_shape=jax.ShapeDtypeStruct((t, d), F32),
        compiler_params=pltpu.CompilerParams(
            dimension_semantics=("arbitrary",), vmem_limit_bytes=_vmem_limit(vmem + (8 << 20))),
        name=name,
    )(*ys, w_bf16, x2, ln_g.reshape(1, d), ln_b.reshape(1, d))


def _rglru_kernel(ax_ref, ag_ref, cw_ref, cb_ref, wg_ref, ba_ref, bx_ref, lam_ref, y_ref,
                  halo_ref, h_ref, *, tc):
    @pl.when(pl.program_id(2) == 0)
    def _():
        halo_ref[...] = jnp.zeros_like(halo_ref)
        h_ref[...] = jnp.zeros_like(h_ref)

    x = ax_ref[...]
    width = x.shape[1]
    halo = halo_ref[...]
    row8 = lax.broadcasted_iota(jnp.int32, (SUBLANES, width), 0)
    cw = cw_ref[...]
    xc = cb_ref[...] + cw[CONV_WIDTH - 1:CONV_WIDTH] * x
    for d in range(1, CONV_WIDTH):
        rolled = pltpu.roll(x, d, axis=0)
        head = jnp.where(row8 < d, pltpu.roll(halo, d, axis=0), rolled[:SUBLANES])
        xd = jnp.concatenate([head, rolled[SUBLANES:]], axis=0)
        xc = xc + cw[CONV_WIDTH - 1 - d:CONV_WIDTH - d] * xd
    halo_ref[...] = x[tc - SUBLANES:]

    xcb = xc.astype(BF16)
    gates = [jnp.dot(xcb[:, n * LRU_BLOCK:(n + 1) * LRU_BLOCK], wg_ref[n],
                     preferred_element_type=F32) for n in range(width // LRU_BLOCK)]
    r = _sigmoid(jnp.concatenate([g[:, :LRU_BLOCK] for g in gates], axis=1) + ba_ref[...])
    i = _sigmoid(jnp.concatenate([g[:, LRU_BLOCK:] for g in gates], axis=1) + bx_ref[...])
    lam = lam_ref[...]
    log_sig_lam = jnp.minimum(lam, 0.0) - jnp.log(1.0 + jnp.exp(-jnp.abs(lam)))
    log_a = (LRU_C * log_sig_lam) * r
    a = jnp.exp(log_a)
    u = jnp.sqrt(1.0 - a * a) * (i * xc)

    rows = lax.broadcasted_iota(jnp.int32, (tc, width), 0)
    d = 1
    while d < tc:
        if d < SUBLANES:
            a_sh = jnp.where(rows >= d, pltpu.roll(a, d, axis=0), 1.0)
            u_sh = jnp.where(rows >= d, pltpu.roll(u, d, axis=0), 0.0)
            u = u + a * u_sh
            a = a * a_sh
        else:
            u_tail = u[d:] + a[d:] * u[:tc - d]
            a_tail = a[d:] * a[:tc - d]
            u = jnp.concatenate([u[:d], u_tail], axis=0)
            a = jnp.concatenate([a[:d], a_tail], axis=0)
        d *= 2
    h = u + a * h_ref[...]
    h_ref[...] = h[tc - 1:tc]
    y_ref[...] = (h * _silu(ag_ref[...].astype(F32))).astype(y_ref.dtype)


def _rglru(a_x, a_g, conv_w, conv_b, w_gates_bf16, b_a, b_x, lam, *, batch, tc, nb):
    t, width = a_x.shape
    seq = t // batch
    cw = nb * LRU_BLOCK
    nblk = width // cw
    nt = seq // tc
    assert seq % tc == 0 and width % cw == 0
    row_blk = lambda b, n, s: (b * nt + s, n)
    par_blk = lambda b, n, s: (0, n)
    return pl.pallas_call(
        functools.partial(_rglru_kernel, tc=tc),
        grid=(batch, nblk, nt),
        in_specs=[
            pl.BlockSpec((tc, cw), row_blk),
            pl.BlockSpec((tc, cw), row_blk),
            pl.BlockSpec((CONV_WIDTH, cw), par_blk),
            pl.BlockSpec((1, cw), par_blk),
            pl.BlockSpec((nb, LRU_BLOCK, 2 * LRU_BLOCK), lambda b, n, s: (n, 0, 0)),
            pl.BlockSpec((1, cw), par_blk),
            pl.BlockSpec((1, cw), par_blk),
            pl.BlockSpec((1, cw), par_blk),
        ],
        out_specs=pl.BlockSpec((tc, cw), row_blk),
        out_shape=jax.ShapeDtypeStruct((t, width), BF16),
        scratch_shapes=[pltpu.VMEM((SUBLANES, cw), F32), pltpu.VMEM((1, cw), F32)],
        compiler_params=pltpu.CompilerParams(
            dimension_semantics=("arbitrary", "arbitrary", "arbitrary"),
            vmem_limit_bytes=_vmem_limit(24 << 20)),
        name="rglru",
    )(a_x, a_g, conv_w, conv_b.reshape(1, width), w_gates_bf16, b_a.reshape(1, width),
      b_x.reshape(1, width), lam.reshape(1, width))


def _stickbreak_kernel(q_ref, k_ref, v_ref, g_ref, o_ref, ntri_ref, z_ref, w_ref, acc_ref,
                       *, tq, tk, nh):
    dh = SB_HEAD_DIM
    i = pl.program_id(2)
    nsub = tq // tk
    chains = [(s, h) for s in range(nsub) for h in range(nh)]

    @pl.when((pl.program_id(0) == 0) & (pl.program_id(1) == 0) & (i == 0))
    def _():
        r = lax.broadcasted_iota(jnp.int32, (tk, tk), 0)
        c = lax.broadcasted_iota(jnp.int32, (tk, tk), 1)
        ntri_ref[...] = jnp.where(r > c, -1.0, 0.0).astype(BF16)

    ntri = ntri_ref[...]

    def key_block(g, s):
        return jnp.maximum((i - g) * nsub + (nsub - 1 - s), 0)

    def scores(g, c):
        s, h = chains[c]
        start = pl.multiple_of(key_block(g, s) * tk, tk)
        kb = k_ref[pl.ds(start, tk), h * dh:(h + 1) * dh]
        z_ref[c] = lax.dot_general(
            q_ref[:, h * dh:(h + 1) * dh], kb, (((1,), (1,)), ((), ())),
            preferred_element_type=F32)

    def weighted_values(g, c):
        s, h = chains[c]
        start = pl.multiple_of(key_block(g, s) * tk, tk)
        vb = v_ref[pl.ds(start, tk), h * dh:(h + 1) * dh]
        acc_ref[:, h * dh:(h + 1) * dh] += jnp.dot(w_ref[c], vb, preferred_element_type=F32)

    def causal_masks(g):
        rows = i * tq + lax.broadcasted_iota(jnp.int32, (tq, tk), 0)
        cols = lax.broadcasted_iota(jnp.int32, (tq, tk), 1)
        return [key_block(g, s) * tk + cols < rows for s in range(nsub)]

    def step(g, carries, causal=None):
        carries = list(carries)
        logits = []
        for c, (s, h) in enumerate(chains):
            z = z_ref[c]
            sp = jnp.maximum(z, 0.0) + jnp.log(1.0 + jnp.exp(-jnp.abs(z)))
            if causal is not None:
                sp = jnp.where(causal[s], sp, 0.0)
            suffix = jnp.dot(sp.astype(BF16), ntri, preferred_element_type=F32)
            logits.append((z - sp) + suffix + carries[h])
            carries[h] = carries[h] - jnp.sum(sp, axis=-1, keepdims=True)
            if causal is None:
                weighted_values(g - 1, c)
            scores(g + 1, c)
        for c, (s, h) in enumerate(chains):
            w = jnp.exp(logits[c])
            if causal is not None:
                w = jnp.where(causal[s], w, 0.0)
            w_ref[c] = w.astype(BF16)
        return tuple(carries)

    acc_ref[...] = jnp.zeros_like(acc_ref)
    for c in range(len(chains)):
        scores(0, c)
    carries = step(0, tuple(jnp.zeros((tq, 1), F32) for _ in range(nh)), causal_masks(0))
    def alive(carries):
        top = carries[0]
        for other in carries[1:]:
            top = jnp.maximum(top, other)
        return (jnp.max(top) > STICK_DEAD).astype(jnp.int32)

    def more(state):
        return jnp.logical_and(state[0] <= i, state[1] > 0)

    def next_group(state):
        carries = step(state[0], state[2:])
        return (state[0] + 1, alive(carries)) + carries

    state = lax.while_loop(more, next_group, (jnp.int32(1), alive(carries)) + carries)
    for c in range(len(chains)):
        weighted_values(state[0] - 1, c)
    o_ref[...] = (acc_ref[...] * _silu(g_ref[...].astype(F32))).astype(o_ref.dtype)


def _stickbreak(q, k, v, g, *, batch, tq, tk, nh):
    t, width = q.shape
    seq = t // batch
    gw = nh * SB_HEAD_DIM
    groups = width // gw
    nq = seq // tq
    assert seq % tq == 0 and tq % tk == 0 and width % gw == 0
    q_blk = lambda b, h, i: (b * nq + i, h)
    kv_blk = lambda b, h, i: (b, h)
    nchain = (tq // tk) * nh
    vmem = 2 * 2 * seq * gw * 2 + nchain * tq * tk * (4 + 2 + 8 * 4)
    return pl.pallas_call(
        functools.partial(_stickbreak_kernel, tq=tq, tk=tk, nh=nh),
        grid=(batch, groups, nq),
        in_specs=[
            pl.BlockSpec((tq, gw), q_blk),
            pl.BlockSpec((seq, gw), kv_blk),
            pl.BlockSpec((seq, gw), kv_blk),
            pl.BlockSpec((tq, gw), q_blk),
        ],
        out_specs=pl.BlockSpec((tq, gw), q_blk),
        out_shape=jax.ShapeDtypeStruct((t, width), BF16),
        scratch_shapes=[
            pltpu.VMEM((tk, tk), BF16),
            pltpu.VMEM((nchain, tq, tk), F32),
            pltpu.VMEM((nchain, tq, tk), BF16),
            pltpu.VMEM((tq, gw), F32),
        ],
        compiler_params=pltpu.CompilerParams(
            dimension_semantics=("arbitrary", "arbitrary", "arbitrary"),
            vmem_limit_bytes=_vmem_limit(vmem + (8 << 20))),
        name="stickbreak",
    )(q, k, v, g)


def _alibi_slope(h):
    return 2.0 ** (-8.0 * (h + 1) / C_HEADS)


def _swa_kernel(sink_ref, q_ref, kp_ref, kc_ref, vp_ref, vc_ref, g_ref, o_ref, *, tq, blocks_per_seq):
    nk = WINDOW + tq
    pairs = C_GROUP // 2
    first = (pl.program_id(0) % blocks_per_seq) == 0

    r = lax.broadcasted_iota(jnp.int32, (tq, nk), 0)
    c = lax.broadcasted_iota(jnp.int32, (tq, nk), 1)
    dist = r - c + WINDOW
    first_key = jnp.where(first, WINDOW, 0)
    valid = (dist >= 0) & (dist < WINDOW) & (c >= first_key)
    mdist = jnp.where(valid, dist.astype(F32), jnp.inf)

    lane = lax.broadcasted_iota(jnp.int32, (nk, LANES), 1)
    low = lane < C_HEAD_DIM

    def halves(prev_ref, cur_ref):
        full = jnp.concatenate([prev_ref[...], cur_ref[...]], axis=0).astype(F32)
        swapped = pltpu.roll(full, C_HEAD_DIM, axis=1)
        zero = jnp.zeros_like(full)
        even = [jnp.where(low, full, zero), jnp.where(low, swapped, zero)]
        odd = [jnp.where(low, zero, swapped), jnp.where(low, zero, full)]
        return [[e.astype(BF16) for e in even], [o.astype(BF16) for o in odd]]

    k_ext = halves(kp_ref, kc_ref)
    v_ext = halves(vp_ref, vc_ref)

    chains = [(kv, parity) for kv in range(C_KV_HEADS) for parity in range(2)]
    cols = [[(kv * pairs + p) * LANES for p in range(pairs)] for kv in range(C_KV_HEADS)]
    scores = []
    for kv, parity in chains:
        qs = jnp.concatenate([q_ref[:, c0:c0 + LANES] for c0 in cols[kv]], axis=0)
        scores.append(lax.dot_general(qs, k_ext[parity][kv], (((1,), (1,)), ((), ())),
                                      preferred_element_type=F32))
    outs = [None] * C_KV_HEADS
    for (kv, parity), s in zip(chains, scores):
        probs, invs = [], []
        for p in range(pairs):
            head = kv * C_GROUP + 2 * p + parity
            sink = sink_ref[head]
            sc = s[p * tq:(p + 1) * tq] - _alibi_slope(head) * mdist
            m = jnp.maximum(jnp.max(sc, axis=-1, keepdims=True), sink)
            pe = jnp.exp(sc - m)
            denom = jnp.sum(pe, axis=-1, keepdims=True) + jnp.exp(sink - m)
            probs.append(pe.astype(BF16))
            invs.append(1.0 / denom)
        o_par = jnp.dot(jnp.concatenate(probs, axis=0), v_ext[parity][kv],
                        preferred_element_type=F32)
        o_par = o_par * jnp.concatenate(invs, axis=0)
        outs[kv] = o_par if outs[kv] is None else outs[kv] + o_par
    for kv in range(C_KV_HEADS):
        for p, c0 in enumerate(cols[kv]):
            gate = _silu(g_ref[:, c0:c0 + LANES].astype(F32))
            o_ref[:, c0:c0 + LANES] = (outs[kv][p * tq:(p + 1) * tq] * gate).astype(o_ref.dtype)


def _swa(q, k, v, g, sinks, *, batch, tq):
    t, width = q.shape
    seq = t // batch
    assert seq % tq == 0 and tq % WINDOW == 0 and k.shape[1] == LANES
    blocks_per_seq = seq // tq
    ratio = tq // WINDOW
    cur = lambda i: (i, 0)
    prev = lambda i: (jnp.maximum(i * ratio - 1, 0), 0)
    return pl.pallas_call(
        functools.partial(_swa_kernel, tq=tq, blocks_per_seq=blocks_per_seq),
        grid=(t // tq,),
        in_specs=[
            pl.BlockSpec(memory_space=pltpu.SMEM),
            pl.BlockSpec((tq, width), cur),
            pl.BlockSpec((WINDOW, LANES), prev),
            pl.BlockSpec((tq, LANES), cur),
            pl.BlockSpec((WINDOW, LANES), prev),
            pl.BlockSpec((tq, LANES), cur),
            pl.BlockSpec((tq, width), cur),
        ],
        out_specs=pl.BlockSpec((tq, width), cur),
        out_shape=jax.ShapeDtypeStruct((t, width), BF16),
        compiler_params=pltpu.CompilerParams(
            dimension_semantics=("arbitrary",), vmem_limit_bytes=_vmem_limit(32 << 20)),
        name="swa",
    )(sinks, q, k, k, v, v, g)


def _even_layer(x2, batch, w_in, conv_w, conv_b, w_gate_a, b_gate_a, w_gate_x, b_gate_x, lam,
                w_out, ln_g, ln_b, alpha):
    d = x2.shape[1]
    a_x, a_g, b_q, b_k, b_v, b_g = _inproj(
        x2, w_in.astype(BF16), widths=[d] * 6, dtypes=[F32, BF16, BF16, BF16, BF16, BF16],
        scales=[1.0, 1.0, 1.0 / math.sqrt(SB_HEAD_DIM), 1.0, 1.0, 1.0], tm=512, name="inproj_even")
    w_gates = jnp.concatenate([w_gate_a, w_gate_x], axis=-1).astype(BF16)
    y_a = _rglru(a_x, a_g, conv_w, conv_b, w_gates, b_gate_a, b_gate_x, lam, batch=batch, tc=256, nb=4)
    y_b = _stickbreak(b_q, b_k, b_v, b_g, batch=batch, tq=512, tk=256, nh=2)
    return _outproj_ln([y_a, y_b], w_out.astype(BF16), x2, ln_g, ln_b, alpha, tm=512,
                       name="outproj_ln_even")


def _odd_layer(x2, batch, w_in, sinks, w_out, ln_g, ln_b, alpha):
    d = x2.shape[1]
    kv_width = C_KV_HEADS * C_HEAD_DIM
    q, k, v, g = _inproj(
        x2, w_in.astype(BF16), widths=[d, kv_width, kv_width, d], dtypes=[BF16] * 4,
        scales=[1.0 / math.sqrt(C_HEAD_DIM), 1.0, 1.0, 1.0], tm=512, name="inproj_odd")
    y = _swa(q, k, v, g, sinks, batch=batch, tq=WINDOW)
    return _outproj_ln([y], w_out.astype(BF16), x2, ln_g, ln_b, alpha, tm=512, name="outproj_ln_odd")


def kernel(x, e_w_in, e_conv_w, e_conv_b, e_w_gate_a, e_b_gate_a, e_w_gate_x, e_b_gate_x, e_lru_lambda, e_w_out, e_ln_g, e_ln_b, o_w_in, o_sinks, o_w_out, o_ln_g, o_ln_b):
    batch, seq, d = x.shape
    depth = e_w_in.shape[0] + o_w_in.shape[0]
    alpha = float((2 * depth) ** 0.25)
    x2 = x.reshape(batch * seq, d)
    for layer in range(depth):
        li = layer // 2
        if layer % 2 == 0:
            x2 = _even_layer(x2, batch, e_w_in[li], e_conv_w[li], e_conv_b[li], e_w_gate_a[li],
                             e_b_gate_a[li], e_w_gate_x[li], e_b_gate_x[li], e_lru_lambda[li],
                             e_w_out[li], e_ln_g[li], e_ln_b[li], alpha)
        else:
            x2 = _odd_layer(x2, batch, o_w_in[li], o_sinks[li], o_w_out[li], o_ln_g[li], o_ln_b[li],
                            alpha)
    return x2.reshape(batch, seq, d)
```

```python
import functools
import math

import jax
import jax.numpy as jnp
from jax import lax
from jax.experimental import pallas as pl
from jax.experimental.pallas import tpu as pltpu

F32 = jnp.float32
BF16 = jnp.bfloat16

LANES = 128
SUBLANES = 8
VMEM_BYTES_V7X = 64 * 1024 * 1024

LRU_BLOCK = 128
LRU_C = 8.0
CONV_WIDTH = 4
SB_HEAD_DIM = 128
C_HEADS = 16
C_KV_HEADS = 2
C_GROUP = C_HEADS // C_KV_HEADS
C_HEAD_DIM = 64
WINDOW = 128
LN_EPS = 1e-5
STICK_DEAD = -100.0
LOG2E = math.log2(math.e)


def _sigmoid(x):
    return 0.5 * jnp.tanh(0.5 * x) + 0.5


def _silu(x):
    return x * _sigmoid(x)


def _vmem_limit(nbytes):
    return int(min(nbytes, VMEM_BYTES_V7X - 8 * 1024 * 1024))


def _inproj_kernel(x_ref, w_ref, *out_refs, widths, scales):
    xb = x_ref[...].astype(BF16)
    col = 0
    for o_ref, width, scale in zip(out_refs, widths, scales):
        r = jnp.dot(xb, w_ref[:, col:col + width], preferred_element_type=F32)
        if scale != 1.0:
            r = r * scale
        o_ref[...] = r.astype(o_ref.dtype)
        col += width


def _inproj(x2, w_bf16, widths, dtypes, scales, *, tm, name):
    t, d = x2.shape
    n = w_bf16.shape[1]
    assert sum(widths) == n and t % tm == 0
    out_bytes = sum(tm * wd * jnp.dtype(dt).itemsize for wd, dt in zip(widths, dtypes))
    vmem = d * n * 2 + 2 * tm * d * 4 + 2 * out_bytes + tm * d * 2 + 3 * tm * max(widths) * 4
    return pl.pallas_call(
        functools.partial(_inproj_kernel, widths=tuple(widths), scales=tuple(scales)),
        grid=(t // tm,),
        in_specs=[
            pl.BlockSpec((tm, d), lambda i: (i, 0)),
            pl.BlockSpec((d, n), lambda i: (0, 0), pipeline_mode=pl.Buffered(1)),
        ],
        out_specs=[pl.BlockSpec((tm, wd), lambda i: (i, 0)) for wd in widths],
        out_shape=[jax.ShapeDtypeStruct((t, wd), dt) for wd, dt in zip(widths, dtypes)],
        compiler_params=pltpu.CompilerParams(
            dimension_semantics=("arbitrary",), vmem_limit_bytes=_vmem_limit(vmem + (8 << 20))),
        name=name,
    )(x2, w_bf16)


def _outproj_ln_kernel(*refs, n_y, alpha):
    y_refs = refs[:n_y]
    w_ref, x_ref, g_ref, b_ref, o_ref = refs[n_y:]
    acc = alpha * x_ref[...]
    row = 0
    for y_ref in y_refs:
        kdim = y_ref.shape[1]
        acc = acc + jnp.dot(y_ref[...], w_ref[row:row + kdim, :], preferred_element_type=F32)
        row += kdim
    mu = jnp.mean(acc, axis=-1, keepdims=True)
    dev = acc - mu
    var = jnp.mean(dev * dev, axis=-1, keepdims=True)
    o_ref[...] = dev * lax.rsqrt(var + LN_EPS) * g_ref[...] + b_ref[...]


def _outproj_ln(ys, w_bf16, x2, ln_g, ln_b, alpha, *, tm, name):
    t, d = x2.shape
    kdim = w_bf16.shape[0]
    assert sum(y.shape[1] for y in ys) == kdim and t % tm == 0
    vmem = kdim * d * 2 + 2 * tm * kdim * 2 + 4 * tm * d * 4 + 4 * tm * d * 4
    return pl.pallas_call(
        functools.partial(_outproj_ln_kernel, n_y=len(ys), alpha=alpha),
        grid=(t // tm,),
        in_specs=(
            [pl.BlockSpec((tm, y.shape[1]), lambda i: (i, 0)) for y in ys]
            + [
                pl.BlockSpec((kdim, d), lambda i: (0, 0), pipeline_mode=pl.Buffered(1)),
                pl.BlockSpec((tm, d), lambda i: (i, 0)),
                pl.BlockSpec((1, d), lambda i: (0, 0)),
                pl.BlockSpec((1, d), lambda i: (0, 0)),
            ]
        ),
        out_specs=pl.BlockSpec((tm, d), lambda i: (i, 0)),
        out_shape=jax.ShapeDtypeStruct((t, d), F32),
        compiler_params=pltpu.CompilerParams(
            dimension_semantics=("arbitrary",), vmem_limit_bytes=_vmem_limit(vmem + (8 << 20))),
        name=name,
    )(*ys, w_bf16, x2, ln_g.reshape(1, d), ln_b.reshape(1, d))


def _rglru_kernel(ax_ref, ag_ref, cw_ref, cb_ref, wg_ref, ba_ref, bx_ref, lam_ref, y_ref,
                  halo_ref, h_ref, *, tc):
    @pl.when(pl.program_id(2) == 0)
    def _():
        halo_ref[...] = jnp.zeros_like(halo_ref)
        h_ref[...] = jnp.zeros_like(h_ref)

    x = ax_ref[...]
    width = x.shape[1]
    halo = halo_ref[...]
    row8 = lax.broadcasted_iota(jnp.int32, (SUBLANES, width), 0)
    cw = cw_ref[...]
    xc = cb_ref[...] + cw[CONV_WIDTH - 1:CONV_WIDTH] * x
    for d in range(1, CONV_WIDTH):
        rolled = pltpu.roll(x, d, axis=0)
        head = jnp.where(row8 < d, pltpu.roll(halo, d, axis=0), rolled[:SUBLANES])
        xd = jnp.concatenate([head, rolled[SUBLANES:]], axis=0)
        xc = xc + cw[CONV_WIDTH - 1 - d:CONV_WIDTH - d] * xd
    halo_ref[...] = x[tc - SUBLANES:]

    xcb = xc.astype(BF16)
    gates = [jnp.dot(xcb[:, n * LRU_BLOCK:(n + 1) * LRU_BLOCK], wg_ref[n],
                     preferred_element_type=F32) for n in range(width // LRU_BLOCK)]
    r = _sigmoid(jnp.concatenate([g[:, :LRU_BLOCK] for g in gates], axis=1) + ba_ref[...])
    i = _sigmoid(jnp.concatenate([g[:, LRU_BLOCK:] for g in gates], axis=1) + bx_ref[...])
    lam = lam_ref[...]
    log_sig_lam = jnp.minimum(lam, 0.0) - jnp.log(1.0 + jnp.exp(-jnp.abs(lam)))
    log_a = (LRU_C * log_sig_lam) * r
    a = jnp.exp(log_a)
    y = 1.0 - a * a
    u = jnp.where(y > 0.0, y * lax.rsqrt(y), 0.0) * (i * xc)

    rows = lax.broadcasted_iota(jnp.int32, (tc, width), 0)
    d = 1
    while d < tc:
        if d < SUBLANES:
            a_sh = jnp.where(rows >= d, pltpu.roll(a, d, axis=0), 1.0)
            u_sh = jnp.where(rows >= d, pltpu.roll(u, d, axis=0), 0.0)
            u = u + a * u_sh
            a = a * a_sh
        else:
            u_tail = u[d:] + a[d:] * u[:tc - d]
            a_tail = a[d:] * a[:tc - d]
            u = jnp.concatenate([u[:d], u_tail], axis=0)
            a = jnp.concatenate([a[:d], a_tail], axis=0)
        d *= 2
    h = u + a * h_ref[...]
    h_ref[...] = h[tc - 1:tc]
    y_ref[...] = (h * _silu(ag_ref[...].astype(F32))).astype(y_ref.dtype)


def _rglru(a_x, a_g, conv_w, conv_b, w_gates_bf16, b_a, b_x, lam, *, batch, tc, nb):
    t, width = a_x.shape
    seq = t // batch
    cw = nb * LRU_BLOCK
    nblk = width // cw
    nt = seq // tc
    assert seq % tc == 0 and width % cw == 0
    row_blk = lambda b, n, s: (b * nt + s, n)
    par_blk = lambda b, n, s: (0, n)
    return pl.pallas_call(
        functools.partial(_rglru_kernel, tc=tc),
        grid=(batch, nblk, nt),
        in_specs=[
            pl.BlockSpec((tc, cw), row_blk),
            pl.BlockSpec((tc, cw), row_blk),
            pl.BlockSpec((CONV_WIDTH, cw), par_blk),
            pl.BlockSpec((1, cw), par_blk),
            pl.BlockSpec((nb, LRU_BLOCK, 2 * LRU_BLOCK), lambda b, n, s: (n, 0, 0)),
            pl.BlockSpec((1, cw), par_blk),
            pl.BlockSpec((1, cw), par_blk),
            pl.BlockSpec((1, cw), par_blk),
        ],
        out_specs=pl.BlockSpec((tc, cw), row_blk),
        out_shape=jax.ShapeDtypeStruct((t, width), BF16),
        scratch_shapes=[pltpu.VMEM((SUBLANES, cw), F32), pltpu.VMEM((1, cw), F32)],
        compiler_params=pltpu.CompilerParams(
            dimension_semantics=("arbitrary", "arbitrary", "arbitrary"),
            vmem_limit_bytes=_vmem_limit(24 << 20)),
        name="rglru",
    )(a_x, a_g, conv_w, conv_b.reshape(1, width), w_gates_bf16, b_a.reshape(1, width),
      b_x.reshape(1, width), lam.reshape(1, width))


def _stickbreak_kernel(q_ref, k_ref, v_ref, g_ref, o_ref, ntri_ref, z_ref, w_ref, acc_ref,
                       *, tq, tk, nh):
    dh = SB_HEAD_DIM
    i = pl.program_id(2)
    nsub = tq // tk
    chains = [(s, h) for s in range(nsub) for h in range(nh)]

    @pl.when((pl.program_id(0) == 0) & (pl.program_id(1) == 0) & (i == 0))
    def _():
        r = lax.broadcasted_iota(jnp.int32, (tk, tk), 0)
        c = lax.broadcasted_iota(jnp.int32, (tk, tk), 1)
        ntri_ref[...] = jnp.where(r > c, -1.0, 0.0).astype(BF16)

    ntri = ntri_ref[...]

    def key_block(g, s):
        return jnp.maximum((i - g) * nsub + (nsub - 1 - s), 0)

    def scores(g, c):
        s, h = chains[c]
        start = pl.multiple_of(key_block(g, s) * tk, tk)
        kb = k_ref[pl.ds(start, tk), h * dh:(h + 1) * dh]
        z_ref[c] = lax.dot_general(
            q_ref[:, h * dh:(h + 1) * dh], kb, (((1,), (1,)), ((), ())),
            preferred_element_type=F32)

    def weighted_values(g, c):
        s, h = chains[c]
        start = pl.multiple_of(key_block(g, s) * tk, tk)
        vb = v_ref[pl.ds(start, tk), h * dh:(h + 1) * dh]
        acc_ref[:, h * dh:(h + 1) * dh] += jnp.dot(w_ref[c], vb, preferred_element_type=F32)

    def causal_masks(g):
        rows = i * tq + lax.broadcasted_iota(jnp.int32, (tq, tk), 0)
        cols = lax.broadcasted_iota(jnp.int32, (tq, tk), 1)
        return [key_block(g, s) * tk + cols < rows for s in range(nsub)]

    def step(g, carries, causal=None):
        carries = list(carries)
        logits = []
        for c, (s, h) in enumerate(chains):
            z = z_ref[c]
            sp = jnp.maximum(z, 0.0) + jnp.log(1.0 + jnp.exp(-jnp.abs(z)))
            if causal is not None:
                sp = jnp.where(causal[s], sp, 0.0)
            suffix = jnp.dot(sp.astype(BF16), ntri, preferred_element_type=F32)
            logits.append((z - sp) + suffix + carries[h])
            carries[h] = carries[h] - jnp.sum(sp, axis=-1, keepdims=True)
            if causal is None:
                weighted_values(g - 1, c)
            scores(g + 1, c)
        for c, (s, h) in enumerate(chains):
            w = jnp.exp(logits[c])
            if causal is not None:
                w = jnp.where(causal[s], w, 0.0)
            w_ref[c] = w.astype(BF16)
        return tuple(carries)

    acc_ref[...] = jnp.zeros_like(acc_ref)
    for c in range(len(chains)):
        scores(0, c)
    carries = step(0, tuple(jnp.zeros((tq, 1), F32) for _ in range(nh)), causal_masks(0))
    def alive(carries):
        top = carries[0]
        for other in carries[1:]:
            top = jnp.maximum(top, other)
        return (jnp.max(top) > STICK_DEAD).astype(jnp.int32)

    def more(state):
        return jnp.logical_and(state[0] <= i, state[1] > 0)

    def next_group(state):
        carries = step(state[0], state[2:])
        return (state[0] + 1, alive(carries)) + carries

    state = lax.while_loop(more, next_group, (jnp.int32(1), alive(carries)) + carries)
    for c in range(len(chains)):
        weighted_values(state[0] - 1, c)
    o_ref[...] = (acc_ref[...] * _silu(g_ref[...].astype(F32))).astype(o_ref.dtype)


def _stickbreak(q, k, v, g, *, batch, tq, tk, nh):
    t, width = q.shape
    seq = t // batch
    gw = nh * SB_HEAD_DIM
    groups = width // gw
    nq = seq // tq
    assert seq % tq == 0 and tq % tk == 0 and width % gw == 0
    q_blk = lambda b, h, i: (b * nq + i, h)
    kv_blk = lambda b, h, i: (b, h)
    nchain = (tq // tk) * nh
    vmem = 2 * seq * gw * 2 + nchain * tq * tk * (4 + 2 + 8 * 4)
    return pl.pallas_call(
        functools.partial(_stickbreak_kernel, tq=tq, tk=tk, nh=nh),
        grid=(batch, groups, nq),
        in_specs=[
            pl.BlockSpec((tq, gw), q_blk),
            pl.BlockSpec((seq, gw), kv_blk, pipeline_mode=pl.Buffered(1)),
            pl.BlockSpec((seq, gw), kv_blk, pipeline_mode=pl.Buffered(1)),
            pl.BlockSpec((tq, gw), q_blk),
        ],
        out_specs=pl.BlockSpec((tq, gw), q_blk),
        out_shape=jax.ShapeDtypeStruct((t, width), BF16),
        scratch_shapes=[
            pltpu.VMEM((tk, tk), BF16),
            pltpu.VMEM((nchain, tq, tk), F32),
            pltpu.VMEM((nchain, tq, tk), BF16),
            pltpu.VMEM((tq, gw), F32),
        ],
        compiler_params=pltpu.CompilerParams(
            dimension_semantics=("arbitrary", "arbitrary", "arbitrary"),
            vmem_limit_bytes=_vmem_limit(vmem + (8 << 20))),
        name="stickbreak",
    )(q, k, v, g)


def _alibi_slope(h):
    return 2.0 ** (-8.0 * (h + 1) / C_HEADS)


def _swa_kernel(sink_ref, q_ref, kp_ref, kc_ref, vp_ref, vc_ref, g_ref, o_ref, *, tq, blocks_per_seq):
    nk = WINDOW + tq
    pairs = C_GROUP // 2
    first = (pl.program_id(0) % blocks_per_seq) == 0

    r = lax.broadcasted_iota(jnp.int32, (tq, nk), 0)
    c = lax.broadcasted_iota(jnp.int32, (tq, nk), 1)
    dist = r - c + WINDOW
    first_key = jnp.where(first, WINDOW, 0)
    valid = (dist >= 0) & (dist < WINDOW) & (c >= first_key)
    mdist = jnp.where(valid, dist.astype(F32), jnp.inf)

    lane = lax.broadcasted_iota(jnp.int32, (nk, LANES), 1)
    low = lane < C_HEAD_DIM

    def halves(prev_ref, cur_ref):
        full = jnp.concatenate([prev_ref[...], cur_ref[...]], axis=0).astype(F32)
        swapped = pltpu.roll(full, C_HEAD_DIM, axis=1)
        zero = jnp.zeros_like(full)
        even = [jnp.where(low, full, zero), jnp.where(low, swapped, zero)]
        odd = [jnp.where(low, zero, swapped), jnp.where(low, zero, full)]
        return [[e.astype(BF16) for e in even], [o.astype(BF16) for o in odd]]

    k_ext = halves(kp_ref, kc_ref)
    v_ext = halves(vp_ref, vc_ref)

    chains = [(kv, parity) for kv in range(C_KV_HEADS) for parity in range(2)]
    cols = [[(kv * pairs + p) * LANES for p in range(pairs)] for kv in range(C_KV_HEADS)]
    scores = []
    for kv, parity in chains:
        qs = jnp.concatenate([q_ref[:, c0:c0 + LANES] for c0 in cols[kv]], axis=0)
        scores.append(lax.dot_general(qs, k_ext[parity][kv], (((1,), (1,)), ((), ())),
                                      preferred_element_type=F32))
    outs = [None] * C_KV_HEADS
    for (kv, parity), s in zip(chains, scores):
        probs, invs = [], []
        for p in range(pairs):
            head = kv * C_GROUP + 2 * p + parity
            sink = sink_ref[head] * LOG2E
            sc = s[p * tq:(p + 1) * tq] - (_alibi_slope(head) * LOG2E) * mdist
            m = jnp.maximum(jnp.max(sc, axis=-1, keepdims=True), sink)
            pe = jnp.exp2(sc - m)
            denom = jnp.sum(pe, axis=-1, keepdims=True) + jnp.exp2(sink - m)
            probs.append(pe.astype(BF16))
            invs.append(1.0 / denom)
        o_par = jnp.dot(jnp.concatenate(probs, axis=0), v_ext[parity][kv],
                        preferred_element_type=F32)
        o_par = o_par * jnp.concatenate(invs, axis=0)
        outs[kv] = o_par if outs[kv] is None else outs[kv] + o_par
    for kv in range(C_KV_HEADS):
        for p, c0 in enumerate(cols[kv]):
            gate = _silu(g_ref[:, c0:c0 + LANES].astype(F32))
            o_ref[:, c0:c0 + LANES] = (outs[kv][p * tq:(p + 1) * tq] * gate).astype(o_ref.dtype)


def _swa(q, k, v, g, sinks, *, batch, tq):
    t, width = q.shape
    seq = t // batch
    assert seq % tq == 0 and tq % WINDOW == 0 and k.shape[1] == LANES
    blocks_per_seq = seq // tq
    ratio = tq // WINDOW
    cur = lambda i: (i, 0)
    prev = lambda i: (jnp.maximum(i * ratio - 1, 0), 0)
    return pl.pallas_call(
        functools.partial(_swa_kernel, tq=tq, blocks_per_seq=blocks_per_seq),
        grid=(t // tq,),
        in_specs=[
            pl.BlockSpec(memory_space=pltpu.SMEM),
            pl.BlockSpec((tq, width), cur),
            pl.BlockSpec((WINDOW, LANES), prev),
            pl.BlockSpec((tq, LANES), cur),
            pl.BlockSpec((WINDOW, LANES), prev),
            pl.BlockSpec((tq, LANES), cur),
            pl.BlockSpec((tq, width), cur),
        ],
        out_specs=pl.BlockSpec((tq, width), cur),
        out_shape=jax.ShapeDtypeStruct((t, width), BF16),
        compiler_params=pltpu.CompilerParams(
            dimension_semantics=("arbitrary",), vmem_limit_bytes=_vmem_limit(32 << 20)),
        name="swa",
    )(sinks, q, k, k, v, v, g)


def _even_layer(x2, batch, w_in, conv_w, conv_b, w_gate_a, b_gate_a, w_gate_x, b_gate_x, lam,
                w_out, ln_g, ln_b, alpha):
    d = x2.shape[1]
    a_x, a_g, b_q, b_k, b_v, b_g = _inproj(
        x2, w_in.astype(BF16), widths=[d] * 6, dtypes=[F32, BF16, BF16, BF16, BF16, BF16],
        scales=[1.0, 1.0, 1.0 / math.sqrt(SB_HEAD_DIM), 1.0, 1.0, 1.0], tm=512, name="inproj_even")
    w_gates = jnp.concatenate([w_gate_a, w_gate_x], axis=-1).astype(BF16)
    y_a = _rglru(a_x, a_g, conv_w, conv_b, w_gates, b_gate_a, b_gate_x, lam, batch=batch, tc=256, nb=4)
    y_b = _stickbreak(b_q, b_k, b_v, b_g, batch=batch, tq=256, tk=256, nh=4)
    return _outproj_ln([y_a, y_b], w_out.astype(BF16), x2, ln_g, ln_b, alpha, tm=512,
                       name="outproj_ln_even")


def _odd_layer(x2, batch, w_in, sinks, w_out, ln_g, ln_b, alpha):
    d = x2.shape[1]
    kv_width = C_KV_HEADS * C_HEAD_DIM
    q, k, v, g = _inproj(
        x2, w_in.astype(BF16), widths=[d, kv_width, kv_width, d], dtypes=[BF16] * 4,
        scales=[LOG2E / math.sqrt(C_HEAD_DIM), 1.0, 1.0, 1.0], tm=512, name="inproj_odd")
    y = _swa(q, k, v, g, sinks, batch=batch, tq=WINDOW)
    return _outproj_ln([y], w_out.astype(BF16), x2, ln_g, ln_b, alpha, tm=512, name="outproj_ln_odd")


def kernel(x, e_w_in, e_conv_w, e_conv_b, e_w_gate_a, e_b_gate_a, e_w_gate_x, e_b_gate_x, e_lru_lambda, e_w_out, e_ln_g, e_ln_b, o_w_in, o_sinks, o_w_out, o_ln_g, o_ln_b):
    batch, seq, d = x.shape
    depth = e_w_in.shape[0] + o_w_in.shape[0]
    alpha = float((2 * depth) ** 0.25)
    x2 = x.reshape(batch * seq, d)
    for layer in range(depth):
        li = layer // 2
        if layer % 2 == 0:
            x2 = _even_layer(x2, batch, e_w_in[li], e_conv_w[li], e_conv_b[li], e_w_gate_a[li],
                             e_b_gate_a[li], e_w_gate_x[li], e_b_gate_x[li], e_lru_lambda[li],
                             e_w_out[li], e_ln_g[li], e_ln_b[li], alpha)
        else:
            x2 = _odd_layer(x2, batch, o_w_in[li], o_sinks[li], o_w_out[li], o_ln_g[li], o_ln_b[li],
                            alpha)
    return x2.reshape(batch, seq, d)
```

```python
import functools
import math

import jax
import jax.numpy as jnp
from jax import lax
from jax.experimental import pallas as pl
from jax.experimental.pallas import tpu as pltpu

F32 = jnp.float32
BF16 = jnp.bfloat16

LANES = 128
SUBLANES = 8
VMEM_BYTES_V7X = 64 * 1024 * 1024

LRU_BLOCK = 128
LRU_C = 8.0
CONV_WIDTH = 4
SB_HEAD_DIM = 128
C_HEADS = 16
C_KV_HEADS = 2
C_GROUP = C_HEADS // C_KV_HEADS
C_HEAD_DIM = 64
WINDOW = 128
LN_EPS = 1e-5
STICK_DEAD = -100.0
LOG2E = math.log2(math.e)


def _sigmoid(x):
    return 0.5 * jnp.tanh(0.5 * x) + 0.5


def _silu(x):
    return x * _sigmoid(x)


def _vmem_limit(nbytes):
    return int(min(nbytes, VMEM_BYTES_V7X - 8 * 1024 * 1024))


def _inproj_kernel(x_ref, w_ref, *out_refs, widths, scales):
    xb = x_ref[...].astype(BF16)
    col = 0
    for o_ref, width, scale in zip(out_refs, widths, scales):
        r = jnp.dot(xb, w_ref[:, col:col + width], preferred_element_type=F32)
        if scale != 1.0:
            r = r * scale
        o_ref[...] = r.astype(o_ref.dtype)
        col += width


def _inproj(x2, w_bf16, widths, dtypes, scales, *, tm, name):
    t, d = x2.shape
    n = w_bf16.shape[1]
    assert sum(widths) == n and t % tm == 0
    out_bytes = sum(tm * wd * jnp.dtype(dt).itemsize for wd, dt in zip(widths, dtypes))
    vmem = d * n * 2 + 2 * tm * d * 4 + 2 * out_bytes + tm * d * 2 + 3 * tm * max(widths) * 4
    return pl.pallas_call(
        functools.partial(_inproj_kernel, widths=tuple(widths), scales=tuple(scales)),
        grid=(t // tm,),
        in_specs=[
            pl.BlockSpec((tm, d), lambda i: (i, 0)),
            pl.BlockSpec((d, n), lambda i: (0, 0), pipeline_mode=pl.Buffered(1)),
        ],
        out_specs=[pl.BlockSpec((tm, wd), lambda i: (i, 0)) for wd in widths],
        out_shape=[jax.ShapeDtypeStruct((t, wd), dt) for wd, dt in zip(widths, dtypes)],
        compiler_params=pltpu.CompilerParams(
            dimension_semantics=("arbitrary",), vmem_limit_bytes=_vmem_limit(vmem + (8 << 20))),
        name=name,
    )(x2, w_bf16)


def _outproj_ln_kernel(*refs, n_y, alpha, next_widths, next_scales):
    y_refs = refs[:n_y]
    w_ref, x_ref, g_ref, b_ref = refs[n_y:n_y + 4]
    rest = refs[n_y + 4:]
    wn_ref = rest[0] if next_widths else None
    o_ref = rest[1] if next_widths else rest[0]
    acc = alpha * x_ref[...]
    row = 0
    for y_ref in y_refs:
        kdim = y_ref.shape[1]
        acc = acc + jnp.dot(y_ref[...], w_ref[row:row + kdim, :], preferred_element_type=F32)
        row += kdim
    mu = jnp.mean(acc, axis=-1, keepdims=True)
    dev = acc - mu
    var = jnp.mean(dev * dev, axis=-1, keepdims=True)
    out = dev * lax.rsqrt(var + LN_EPS) * g_ref[...] + b_ref[...]
    o_ref[...] = out
    if next_widths:
        ob = out.astype(BF16)
        col = 0
        for n_ref, width, scale in zip(rest[2:], next_widths, next_scales):
            r = jnp.dot(ob, wn_ref[:, col:col + width], preferred_element_type=F32)
            if scale != 1.0:
                r = r * scale
            n_ref[...] = r.astype(n_ref.dtype)
            col += width


def _outproj_ln(ys, w_bf16, x2, ln_g, ln_b, alpha, *, tm, name, next_proj=None):
    t, d = x2.shape
    kdim = w_bf16.shape[0]
    assert sum(y.shape[1] for y in ys) == kdim and t % tm == 0
    resident = lambda shape: pl.BlockSpec(shape, lambda i: (0, 0), pipeline_mode=pl.Buffered(1))
    rows = lambda width: pl.BlockSpec((tm, width), lambda i: (i, 0))
    vmem = kdim * d * 2 + 2 * tm * kdim * 2 + 4 * tm * d * 4 + 4 * tm * d * 4
    in_specs = [rows(y.shape[1]) for y in ys] + [resident((kdim, d)), rows(d), resident((1, d)),
                                                 resident((1, d))]
    operands = [*ys, w_bf16, x2, ln_g.reshape(1, d), ln_b.reshape(1, d)]
    out_specs = [rows(d)]
    out_shape = [jax.ShapeDtypeStruct((t, d), F32)]
    widths, scales = (), ()
    if next_proj is not None:
        w_next, widths, dtypes, scales = next_proj
        assert w_next.shape == (d, sum(widths))
        in_specs.append(resident(w_next.shape))
        operands.append(w_next)
        out_specs += [rows(wd) for wd in widths]
        out_shape += [jax.ShapeDtypeStruct((t, wd), dt) for wd, dt in zip(widths, dtypes)]
        vmem += d * sum(widths) * 2 + tm * d * 2 + 3 * tm * max(widths) * 4
        vmem += 2 * sum(tm * wd * jnp.dtype(dt).itemsize for wd, dt in zip(widths, dtypes))
    outs = pl.pallas_call(
        functools.partial(_outproj_ln_kernel, n_y=len(ys), alpha=alpha,
                          next_widths=tuple(widths), next_scales=tuple(scales)),
        grid=(t // tm,),
        in_specs=in_specs,
        out_specs=out_specs,
        out_shape=out_shape,
        compiler_params=pltpu.CompilerParams(
            dimension_semantics=("arbitrary",), vmem_limit_bytes=_vmem_limit(vmem + (8 << 20))),
        name=name,
    )(*operands)
    return outs[0] if next_proj is None else (outs[0], outs[1:])


def _rglru_kernel(ax_ref, ag_ref, cw_ref, cb_ref, wg_ref, ba_ref, bx_ref, lam_ref, y_ref,
                  halo_ref, h_ref, *, tc):
    @pl.when(pl.program_id(2) == 0)
    def _():
        halo_ref[...] = jnp.zeros_like(halo_ref)
        h_ref[...] = jnp.zeros_like(h_ref)

    x = ax_ref[...]
    width = x.shape[1]
    halo = halo_ref[...]
    row8 = lax.broadcasted_iota(jnp.int32, (SUBLANES, width), 0)
    cw = cw_ref[...]
    xc = cb_ref[...] + cw[CONV_WIDTH - 1:CONV_WIDTH] * x
    for d in range(1, CONV_WIDTH):
        rolled = pltpu.roll(x, d, axis=0)
        head = jnp.where(row8 < d, pltpu.roll(halo, d, axis=0), rolled[:SUBLANES])
        xd = jnp.concatenate([head, rolled[SUBLANES:]], axis=0)
        xc = xc + cw[CONV_WIDTH - 1 - d:CONV_WIDTH - d] * xd
    halo_ref[...] = x[tc - SUBLANES:]

    xcb = xc.astype(BF16)
    gates = [jnp.dot(xcb[:, n * LRU_BLOCK:(n + 1) * LRU_BLOCK], wg_ref[n],
                     preferred_element_type=F32) for n in range(width // LRU_BLOCK)]
    r = _sigmoid(jnp.concatenate([g[:, :LRU_BLOCK] for g in gates], axis=1) + ba_ref[...])
    i = _sigmoid(jnp.concatenate([g[:, LRU_BLOCK:] for g in gates], axis=1) + bx_ref[...])
    lam = lam_ref[...]
    log_sig_lam = jnp.minimum(lam, 0.0) - jnp.log(1.0 + jnp.exp(-jnp.abs(lam)))
    log_a = (LRU_C * log_sig_lam) * r
    a = jnp.exp(log_a)
    y = 1.0 - a * a
    u = jnp.where(y > 0.0, y * lax.rsqrt(y), 0.0) * (i * xc)

    rows = lax.broadcasted_iota(jnp.int32, (tc, width), 0)
    d = 1
    while d < tc:
        if d < SUBLANES:
            a_sh = jnp.where(rows >= d, pltpu.roll(a, d, axis=0), 1.0)
            u_sh = jnp.where(rows >= d, pltpu.roll(u, d, axis=0), 0.0)
            u = u + a * u_sh
            a = a * a_sh
        else:
            u_tail = u[d:] + a[d:] * u[:tc - d]
            a_tail = a[d:] * a[:tc - d]
            u = jnp.concatenate([u[:d], u_tail], axis=0)
            a = jnp.concatenate([a[:d], a_tail], axis=0)
        d *= 2
    h = u + a * h_ref[...]
    h_ref[...] = h[tc - 1:tc]
    y_ref[...] = (h * _silu(ag_ref[...].astype(F32))).astype(y_ref.dtype)


def _rglru(a_x, a_g, conv_w, conv_b, w_gates_bf16, b_a, b_x, lam, *, batch, tc, nb):
    t, width = a_x.shape
    seq = t // batch
    cw = nb * LRU_BLOCK
    nblk = width // cw
    nt = seq // tc
    assert seq % tc == 0 and width % cw == 0
    row_blk = lambda b, n, s: (b * nt + s, n)
    par_blk = lambda b, n, s: (0, n)
    return pl.pallas_call(
        functools.partial(_rglru_kernel, tc=tc),
        grid=(batch, nblk, nt),
        in_specs=[
            pl.BlockSpec((tc, cw), row_blk),
            pl.BlockSpec((tc, cw), row_blk),
            pl.BlockSpec((CONV_WIDTH, cw), par_blk),
            pl.BlockSpec((1, cw), par_blk),
            pl.BlockSpec((nb, LRU_BLOCK, 2 * LRU_BLOCK), lambda b, n, s: (n, 0, 0)),
            pl.BlockSpec((1, cw), par_blk),
            pl.BlockSpec((1, cw), par_blk),
            pl.BlockSpec((1, cw), par_blk),
        ],
        out_specs=pl.BlockSpec((tc, cw), row_blk),
        out_shape=jax.ShapeDtypeStruct((t, width), BF16),
        scratch_shapes=[pltpu.VMEM((SUBLANES, cw), F32), pltpu.VMEM((1, cw), F32)],
        compiler_params=pltpu.CompilerParams(
            dimension_semantics=("arbitrary", "arbitrary", "arbitrary"),
            vmem_limit_bytes=_vmem_limit(24 << 20)),
        name="rglru",
    )(a_x, a_g, conv_w, conv_b.reshape(1, width), w_gates_bf16, b_a.reshape(1, width),
      b_x.reshape(1, width), lam.reshape(1, width))


def _stickbreak_kernel(q_ref, k_ref, v_ref, g_ref, o_ref, ntri_ref, acc_ref, *, tq, nh):
    dh = SB_HEAD_DIM
    i = pl.program_id(2)

    @pl.when((pl.program_id(0) == 0) & (pl.program_id(1) == 0) & (i == 0))
    def _():
        r = lax.broadcasted_iota(jnp.int32, (tq, tq), 0)
        c = lax.broadcasted_iota(jnp.int32, (tq, tq), 1)
        ntri_ref[...] = jnp.where(r > c, -1.0, 0.0).astype(BF16)

    ntri = ntri_ref[...]

    def sweep(blocks, carries, first):
        chains = [(b, h) for b in range(len(blocks)) for h in range(nh)]
        starts = [pl.multiple_of(j * tq, tq) for j, _ in blocks]
        carries = list(carries)
        zs = []
        for b, h in chains:
            kb = k_ref[pl.ds(starts[b], tq), h * dh:(h + 1) * dh]
            zs.append(lax.dot_general(q_ref[:, h * dh:(h + 1) * dh], kb, (((1,), (1,)), ((), ())),
                                      preferred_element_type=F32))
        logits = []
        for (b, h), z in zip(chains, zs):
            mask = blocks[b][1]
            sp = jnp.maximum(z, 0.0) + jnp.log(1.0 + jnp.exp(-jnp.abs(z)))
            if mask is not None:
                sp = jnp.where(mask, sp, 0.0)
            suffix = jnp.dot(sp.astype(BF16), ntri, preferred_element_type=F32)
            logits.append((z - sp) + suffix + carries[h])
            carries[h] = carries[h] - jnp.sum(sp, axis=-1, keepdims=True)
        outs = [None] * nh
        for (b, h), lg in zip(chains, logits):
            mask = blocks[b][1]
            w = jnp.exp(lg)
            if mask is not None:
                w = jnp.where(mask, w, 0.0)
            vb = v_ref[pl.ds(starts[b], tq), h * dh:(h + 1) * dh]
            pv = jnp.dot(w.astype(BF16), vb, preferred_element_type=F32)
            outs[h] = pv if outs[h] is None else outs[h] + pv
        for h in range(nh):
            if first:
                acc_ref[:, h * dh:(h + 1) * dh] = outs[h]
            else:
                acc_ref[:, h * dh:(h + 1) * dh] += outs[h]
        return tuple(carries)

    def alive(carries):
        top = carries[0]
        for other in carries[1:]:
            top = jnp.maximum(top, other)
        return (jnp.max(top) > STICK_DEAD).astype(jnp.int32)

    r = lax.broadcasted_iota(jnp.int32, (tq, tq), 0)
    c = lax.broadcasted_iota(jnp.int32, (tq, tq), 1)
    causal = c < r
    has_prev = (jnp.zeros((tq, tq), jnp.int32) + i) >= 1
    carries = sweep([(i, causal), (jnp.maximum(i - 1, 0), has_prev)],
                    tuple(jnp.zeros((tq, 1), F32) for _ in range(nh)), True)

    def more(state):
        return jnp.logical_and(state[0] >= 0, state[1] > 0)

    def earlier_block(state):
        carries = sweep([(state[0], None)], state[2:], False)
        return (state[0] - 1, alive(carries)) + carries

    lax.while_loop(more, earlier_block, (i - 2, alive(carries)) + carries)
    o_ref[...] = (acc_ref[...] * _silu(g_ref[...].astype(F32))).astype(o_ref.dtype)


def _stickbreak(q, k, v, g, *, batch, tq, nh):
    t, width = q.shape
    seq = t // batch
    gw = nh * SB_HEAD_DIM
    groups = width // gw
    nq = seq // tq
    assert seq % tq == 0 and width % gw == 0
    q_blk = lambda b, h, i: (b * nq + i, h)
    kv_blk = lambda b, h, i: (b, h)
    vmem = 2 * 2 * seq * gw * 2 + 2 * nh * tq * tq * 8 * 4
    return pl.pallas_call(
        functools.partial(_stickbreak_kernel, tq=tq, nh=nh),
        grid=(batch, groups, nq),
        in_specs=[
            pl.BlockSpec((tq, gw), q_blk),
            pl.BlockSpec((seq, gw), kv_blk),
            pl.BlockSpec((seq, gw), kv_blk),
            pl.BlockSpec((tq, gw), q_blk),
        ],
        out_specs=pl.BlockSpec((tq, gw), q_blk),
        out_shape=jax.ShapeDtypeStruct((t, width), BF16),
        scratch_shapes=[
            pltpu.VMEM((tq, tq), BF16),
            pltpu.VMEM((tq, gw), F32),
        ],
        compiler_params=pltpu.CompilerParams(
            dimension_semantics=("arbitrary", "arbitrary", "arbitrary"),
            vmem_limit_bytes=_vmem_limit(vmem + (8 << 20))),
        name="stickbreak",
    )(q, k, v, g)


def _alibi_slope(h):
    return 2.0 ** (-8.0 * (h + 1) / C_HEADS)


def _swa_kernel(sink_ref, q_ref, kp_ref, kc_ref, vp_ref, vc_ref, g_ref, o_ref, *, tq, blocks_per_seq):
    nk = WINDOW + tq
    pairs = C_GROUP // 2
    first = (pl.program_id(0) % blocks_per_seq) == 0

    r = lax.broadcasted_iota(jnp.int32, (tq, nk), 0)
    c = lax.broadcasted_iota(jnp.int32, (tq, nk), 1)
    dist = r - c + WINDOW
    first_key = jnp.where(first, WINDOW, 0)
    valid = (dist >= 0) & (dist < WINDOW) & (c >= first_key)
    mdist = jnp.where(valid, dist.astype(F32), jnp.inf)

    lane = lax.broadcasted_iota(jnp.int32, (nk, LANES), 1)
    low = lane < C_HEAD_DIM

    def halves(prev_ref, cur_ref):
        full = jnp.concatenate([prev_ref[...], cur_ref[...]], axis=0).astype(F32)
        swapped = pltpu.roll(full, C_HEAD_DIM, axis=1)
        zero = jnp.zeros_like(full)
        even = [jnp.where(low, full, zero), jnp.where(low, swapped, zero)]
        odd = [jnp.where(low, zero, swapped), jnp.where(low, zero, full)]
        return [[e.astype(BF16) for e in even], [o.astype(BF16) for o in odd]]

    k_ext = halves(kp_ref, kc_ref)
    v_ext = halves(vp_ref, vc_ref)

    chains = [(kv, parity) for kv in range(C_KV_HEADS) for parity in range(2)]
    cols = [[(kv * pairs + p) * LANES for p in range(pairs)] for kv in range(C_KV_HEADS)]
    scores = []
    for kv, parity in chains:
        qs = jnp.concatenate([q_ref[:, c0:c0 + LANES] for c0 in cols[kv]], axis=0)
        scores.append(lax.dot_general(qs, k_ext[parity][kv], (((1,), (1,)), ((), ())),
                                      preferred_element_type=F32))
    outs = [None] * C_KV_HEADS
    for (kv, parity), s in zip(chains, scores):
        probs, invs = [], []
        for p in range(pairs):
            head = kv * C_GROUP + 2 * p + parity
            sink = sink_ref[head] * LOG2E
            sc = s[p * tq:(p + 1) * tq] - (_alibi_slope(head) * LOG2E) * mdist
            m = jnp.maximum(jnp.max(sc, axis=-1, keepdims=True), sink)
            pe = jnp.exp2(sc - m)
            denom = jnp.sum(pe, axis=-1, keepdims=True) + jnp.exp2(sink - m)
            probs.append(pe.astype(BF16))
            invs.append(1.0 / denom)
        o_par = jnp.dot(jnp.concatenate(probs, axis=0), v_ext[parity][kv],
                        preferred_element_type=F32)
        o_par = o_par * jnp.concatenate(invs, axis=0)
        outs[kv] = o_par if outs[kv] is None else outs[kv] + o_par
    for kv in range(C_KV_HEADS):
        for p, c0 in enumerate(cols[kv]):
            gate = _silu(g_ref[:, c0:c0 + LANES].astype(F32))
            o_ref[:, c0:c0 + LANES] = (outs[kv][p * tq:(p + 1) * tq] * gate).astype(o_ref.dtype)


def _swa(q, k, v, g, sinks, *, batch, tq):
    t, width = q.shape
    seq = t // batch
    assert seq % tq == 0 and tq % WINDOW == 0 and k.shape[1] == LANES
    blocks_per_seq = seq // tq
    ratio = tq // WINDOW
    cur = lambda i: (i, 0)
    prev = lambda i: (jnp.maximum(i * ratio - 1, 0), 0)
    return pl.pallas_call(
        functools.partial(_swa_kernel, tq=tq, blocks_per_seq=blocks_per_seq),
        grid=(t // tq,),
        in_specs=[
            pl.BlockSpec(memory_space=pltpu.SMEM),
            pl.BlockSpec((tq, width), cur),
            pl.BlockSpec((WINDOW, LANES), prev),
            pl.BlockSpec((tq, LANES), cur),
            pl.BlockSpec((WINDOW, LANES), prev),
            pl.BlockSpec((tq, LANES), cur),
            pl.BlockSpec((tq, width), cur),
        ],
        out_specs=pl.BlockSpec((tq, width), cur),
        out_shape=jax.ShapeDtypeStruct((t, width), BF16),
        compiler_params=pltpu.CompilerParams(
            dimension_semantics=("arbitrary",), vmem_limit_bytes=_vmem_limit(32 << 20)),
        name="swa",
    )(sinks, q, k, k, v, v, g)


def _odd_inproj_spec(d, w_in):
    kv_width = C_KV_HEADS * C_HEAD_DIM
    return (w_in.astype(BF16), [d, kv_width, kv_width, d], [BF16] * 4,
            [LOG2E / math.sqrt(C_HEAD_DIM), 1.0, 1.0, 1.0])


def _even_layer(x2, batch, w_in, conv_w, conv_b, w_gate_a, b_gate_a, w_gate_x, b_gate_x, lam,
                w_out, ln_g, ln_b, alpha, next_proj):
    d = x2.shape[1]
    a_x, a_g, b_q, b_k, b_v, b_g = _inproj(
        x2, w_in.astype(BF16), widths=[d] * 6, dtypes=[F32, BF16, BF16, BF16, BF16, BF16],
        scales=[1.0, 1.0, 1.0 / math.sqrt(SB_HEAD_DIM), 1.0, 1.0, 1.0], tm=512, name="inproj_even")
    w_gates = jnp.concatenate([w_gate_a, w_gate_x], axis=-1).astype(BF16)
    y_a = _rglru(a_x, a_g, conv_w, conv_b, w_gates, b_gate_a, b_gate_x, lam, batch=batch, tc=256, nb=4)
    y_b = _stickbreak(b_q, b_k, b_v, b_g, batch=batch, tq=256, nh=4)
    return _outproj_ln([y_a, y_b], w_out.astype(BF16), x2, ln_g, ln_b, alpha, tm=512,
                       name="outproj_ln_even", next_proj=next_proj)


def _odd_layer(x2, proj, batch, w_in, sinks, w_out, ln_g, ln_b, alpha):
    if proj is None:
        w_bf16, widths, dtypes, scales = _odd_inproj_spec(x2.shape[1], w_in)
        proj = _inproj(x2, w_bf16, widths=widths, dtypes=dtypes, scales=scales, tm=512,
                       name="inproj_odd")
    q, k, v, g = proj
    y = _swa(q, k, v, g, sinks, batch=batch, tq=WINDOW)
    return _outproj_ln([y], w_out.astype(BF16), x2, ln_g, ln_b, alpha, tm=512, name="outproj_ln_odd")


def kernel(x, e_w_in, e_conv_w, e_conv_b, e_w_gate_a, e_b_gate_a, e_w_gate_x, e_b_gate_x, e_lru_lambda, e_w_out, e_ln_g, e_ln_b, o_w_in, o_sinks, o_w_out, o_ln_g, o_ln_b):
    batch, seq, d = x.shape
    depth = e_w_in.shape[0] + o_w_in.shape[0]
    alpha = float((2 * depth) ** 0.25)
    x2 = x.reshape(batch * seq, d)
    proj = None
    for layer in range(depth):
        li = layer // 2
        if layer % 2 == 0:
            next_proj = _odd_inproj_spec(d, o_w_in[li]) if layer + 1 < depth else None
            res = _even_layer(x2, batch, e_w_in[li], e_conv_w[li], e_conv_b[li], e_w_gate_a[li],
                              e_b_gate_a[li], e_w_gate_x[li], e_b_gate_x[li], e_lru_lambda[li],
                              e_w_out[li], e_ln_g[li], e_ln_b[li], alpha, next_proj)
            x2, proj = res if next_proj is not None else (res, None)
        else:
            x2 = _odd_layer(x2, proj, batch, o_w_in[li], o_sinks[li], o_w_out[li], o_ln_g[li],
                            o_ln_b[li], alpha)
            proj = None
    return x2.reshape(batch, seq, d)
```

```python
import functools
import math

import jax
import jax.numpy as jnp
from jax import lax
from jax.experimental import pallas as pl
from jax.experimental.pallas import tpu as pltpu

F32 = jnp.float32
BF16 = jnp.bfloat16

LANES = 128
SUBLANES = 8
VMEM_BYTES_V7X = 64 * 1024 * 1024

LRU_BLOCK = 128
LRU_C = 8.0
CONV_WIDTH = 4
SB_HEAD_DIM = 128
C_HEADS = 16
C_KV_HEADS = 2
C_GROUP = C_HEADS // C_KV_HEADS
C_HEAD_DIM = 64
WINDOW = 128
LN_EPS = 1e-5
STICK_DEAD = -100.0
LOG2E = math.log2(math.e)


def _sigmoid(x):
    return 0.5 * jnp.tanh(0.5 * x) + 0.5


def _silu(x):
    return x * _sigmoid(x)


def _vmem_limit(nbytes):
    return int(min(nbytes, VMEM_BYTES_V7X - 8 * 1024 * 1024))


def _inproj_kernel(x_ref, w_ref, *out_refs, widths, scales):
    xb = x_ref[...].astype(BF16)
    col = 0
    for o_ref, width, scale in zip(out_refs, widths, scales):
        r = jnp.dot(xb, w_ref[:, col:col + width], preferred_element_type=F32)
        if scale != 1.0:
            r = r * scale
        o_ref[...] = r.astype(o_ref.dtype)
        col += width


def _inproj(x2, w_bf16, widths, dtypes, scales, *, tm, name):
    t, d = x2.shape
    n = w_bf16.shape[1]
    assert sum(widths) == n and t % tm == 0
    out_bytes = sum(tm * wd * jnp.dtype(dt).itemsize for wd, dt in zip(widths, dtypes))
    vmem = d * n * 2 + 2 * tm * d * 4 + 2 * out_bytes + tm * d * 2 + 3 * tm * max(widths) * 4
    return pl.pallas_call(
        functools.partial(_inproj_kernel, widths=tuple(widths), scales=tuple(scales)),
        grid=(t // tm,),
        in_specs=[
            pl.BlockSpec((tm, d), lambda i: (i, 0)),
            pl.BlockSpec((d, n), lambda i: (0, 0), pipeline_mode=pl.Buffered(1)),
        ],
        out_specs=[pl.BlockSpec((tm, wd), lambda i: (i, 0)) for wd in widths],
        out_shape=[jax.ShapeDtypeStruct((t, wd), dt) for wd, dt in zip(widths, dtypes)],
        compiler_params=pltpu.CompilerParams(
            dimension_semantics=("arbitrary",), vmem_limit_bytes=_vmem_limit(vmem + (8 << 20))),
        name=name,
    )(x2, w_bf16)


def _outproj_ln_kernel(*refs, n_y, alpha, next_widths, next_scales):
    y_refs = refs[:n_y]
    w_ref, x_ref, g_ref, b_ref = refs[n_y:n_y + 4]
    rest = refs[n_y + 4:]
    wn_ref = rest[0] if next_widths else None
    o_ref = rest[1] if next_widths else rest[0]
    acc = alpha * x_ref[...]
    row = 0
    for y_ref in y_refs:
        kdim = y_ref.shape[1]
        acc = acc + jnp.dot(y_ref[...], w_ref[row:row + kdim, :], preferred_element_type=F32)
        row += kdim
    mu = jnp.mean(acc, axis=-1, keepdims=True)
    dev = acc - mu
    var = jnp.mean(dev * dev, axis=-1, keepdims=True)
    out = dev * lax.rsqrt(var + LN_EPS) * g_ref[...] + b_ref[...]
    o_ref[...] = out
    if next_widths:
        ob = out.astype(BF16)
        col = 0
        for n_ref, width, scale in zip(rest[2:], next_widths, next_scales):
            r = jnp.dot(ob, wn_ref[:, col:col + width], preferred_element_type=F32)
            if scale != 1.0:
                r = r * scale
            n_ref[...] = r.astype(n_ref.dtype)
            col += width


def _outproj_ln(ys, w_bf16, x2, ln_g, ln_b, alpha, *, tm, name, next_proj=None):
    t, d = x2.shape
    kdim = w_bf16.shape[0]
    assert sum(y.shape[1] for y in ys) == kdim and t % tm == 0
    resident = lambda shape: pl.BlockSpec(shape, lambda i: (0, 0), pipeline_mode=pl.Buffered(1))
    rows = lambda width: pl.BlockSpec((tm, width), lambda i: (i, 0))
    vmem = kdim * d * 2 + 2 * tm * kdim * 2 + 4 * tm * d * 4 + 4 * tm * d * 4
    in_specs = [rows(y.shape[1]) for y in ys] + [resident((kdim, d)), rows(d), resident((1, d)),
                                                 resident((1, d))]
    operands = [*ys, w_bf16, x2, ln_g.reshape(1, d), ln_b.reshape(1, d)]
    out_specs = [rows(d)]
    out_shape = [jax.ShapeDtypeStruct((t, d), F32)]
    widths, scales = (), ()
    if next_proj is not None:
        w_next, widths, dtypes, scales = next_proj
        assert w_next.shape == (d, sum(widths))
        in_specs.append(resident(w_next.shape))
        operands.append(w_next)
        out_specs += [rows(wd) for wd in widths]
        out_shape += [jax.ShapeDtypeStruct((t, wd), dt) for wd, dt in zip(widths, dtypes)]
        vmem += d * sum(widths) * 2 + tm * d * 2 + 3 * tm * max(widths) * 4
        vmem += 2 * sum(tm * wd * jnp.dtype(dt).itemsize for wd, dt in zip(widths, dtypes))
    outs = pl.pallas_call(
        functools.partial(_outproj_ln_kernel, n_y=len(ys), alpha=alpha,
                          next_widths=tuple(widths), next_scales=tuple(scales)),
        grid=(t // tm,),
        in_specs=in_specs,
        out_specs=out_specs,
        out_shape=out_shape,
        compiler_params=pltpu.CompilerParams(
            dimension_semantics=("arbitrary",), vmem_limit_bytes=_vmem_limit(vmem + (8 << 20))),
        name=name,
    )(*operands)
    return outs[0] if next_proj is None else (outs[0], outs[1:])


def _rglru_block(xc, ag, gates, ba, bx, lam, h_prev, tc):
    r = _sigmoid(gates[:, :LRU_BLOCK] + ba)
    i = _sigmoid(gates[:, LRU_BLOCK:] + bx)
    log_sig_lam = jnp.minimum(lam, 0.0) - jnp.log(1.0 + jnp.exp(-jnp.abs(lam)))
    log_a = (LRU_C * log_sig_lam) * r
    a = jnp.exp(log_a)
    y = 1.0 - a * a
    u = jnp.where(y > 0.0, y * lax.rsqrt(y), 0.0) * (i * xc)
    rows = lax.broadcasted_iota(jnp.int32, (tc, LRU_BLOCK), 0)
    d = 1
    while d < tc:
        if d < SUBLANES:
            a_sh = jnp.where(rows >= d, pltpu.roll(a, d, axis=0), 1.0)
            u_sh = jnp.where(rows >= d, pltpu.roll(u, d, axis=0), 0.0)
            u = u + a * u_sh
            a = a * a_sh
        else:
            u_tail = u[d:] + a[d:] * u[:tc - d]
            a_tail = a[d:] * a[:tc - d]
            u = jnp.concatenate([u[:d], u_tail], axis=0)
            a = jnp.concatenate([a[:d], a_tail], axis=0)
        d *= 2
    h = u + a * h_prev
    return h * _silu(ag), h[tc - 1:tc]


def _conv_block(x, halo, cw, cb, tc):
    row8 = lax.broadcasted_iota(jnp.int32, (SUBLANES, LRU_BLOCK), 0)
    xc = cb + cw[CONV_WIDTH - 1:CONV_WIDTH] * x
    for d in range(1, CONV_WIDTH):
        rolled = pltpu.roll(x, d, axis=0)
        head = jnp.where(row8 < d, pltpu.roll(halo, d, axis=0), rolled[:SUBLANES])
        xd = jnp.concatenate([head, rolled[SUBLANES:]], axis=0)
        xc = xc + cw[CONV_WIDTH - 1 - d:CONV_WIDTH - d] * xd
    return xc


def _inproj_rglru_kernel(x_ref, w_ref, cw_ref, cb_ref, wg_ref, ba_ref, bx_ref, lam_ref,
                         ya_ref, *rest, width, out_widths, out_scales):
    out_refs, (halo_ref, h_ref, ax_ref, ag_ref) = rest[:-4], rest[-4:]
    tc = x_ref.shape[0]
    s = pl.program_id(1)
    nblk = width // LRU_BLOCK
    chunk = 2 * LRU_BLOCK

    @pl.when(s == 0)
    def _():
        ax_ref[...] = jnp.zeros_like(ax_ref)
        ag_ref[...] = jnp.zeros_like(ag_ref)

    @pl.when(s <= 1)
    def _():
        halo_ref[...] = jnp.zeros_like(halo_ref)
        h_ref[...] = jnp.zeros_like(h_ref)

    ax_prev = ax_ref[...]
    ag_prev = ag_ref[...]
    xb = x_ref[...].astype(BF16)
    n_chunks = (2 * width + sum(out_widths)) // chunk
    per_block = -(-n_chunks // nblk)

    def project(c):
        col = c * chunk
        r = jnp.dot(xb, w_ref[:, col:col + chunk], preferred_element_type=F32)
        if col < width:
            ax_ref[:, col:col + chunk] = r
        elif col < 2 * width:
            ag_ref[:, col - width:col - width + chunk] = r
        else:
            off, idx = col - 2 * width, 0
            while off >= out_widths[idx]:
                off -= out_widths[idx]
                idx += 1
            if out_scales[idx] != 1.0:
                r = r * out_scales[idx]
            out_refs[idx][:, off:off + chunk] = r.astype(out_refs[idx].dtype)

    for n in range(nblk):
        blk = slice(n * LRU_BLOCK, (n + 1) * LRU_BLOCK)
        x = ax_prev[:, blk]
        xc = _conv_block(x, halo_ref[:, blk], cw_ref[:, blk], cb_ref[:, blk], tc)
        halo_ref[:, blk] = x[tc - SUBLANES:]
        gates = jnp.dot(xc.astype(BF16), wg_ref[n], preferred_element_type=F32)
        for c in range(n * per_block, min((n + 1) * per_block, n_chunks)):
            project(c)
        y, h_last = _rglru_block(xc, ag_prev[:, blk], gates, ba_ref[:, blk], bx_ref[:, blk],
                                 lam_ref[:, blk], h_ref[:, blk], tc)
        h_ref[:, blk] = h_last
        ya_ref[:, blk] = y.astype(ya_ref.dtype)


def _inproj_rglru(x2, w_bf16, conv_w, conv_b, w_gates_bf16, b_a, b_x, lam, out_widths, out_scales,
                  *, batch, tc):
    t, d = x2.shape
    seq = t // batch
    width = conv_w.shape[1]
    nt = seq // tc
    n = w_bf16.shape[1]
    assert seq % tc == 0 and n == 2 * width + sum(out_widths)
    cur = lambda wd: pl.BlockSpec((tc, wd), lambda b, s: (b * nt + jnp.minimum(s, nt - 1), 0))
    lag = lambda wd: pl.BlockSpec((tc, wd), lambda b, s: (b * nt + jnp.maximum(s - 1, 0), 0))
    resident = lambda shape: pl.BlockSpec(shape, lambda b, s: (0,) * len(shape),
                                          pipeline_mode=pl.Buffered(1))
    vmem = d * n * 2 + 2 * tc * d * 4 + 2 * tc * (width + sum(out_widths)) * 2 + 24 * tc * width * 4
    outs = pl.pallas_call(
        functools.partial(_inproj_rglru_kernel, width=width, out_widths=tuple(out_widths),
                          out_scales=tuple(out_scales)),
        grid=(batch, nt + 1),
        in_specs=[cur(d), resident((d, n)), resident(conv_w.shape), resident((1, width)),
                  resident(w_gates_bf16.shape), resident((1, width)), resident((1, width)),
                  resident((1, width))],
        out_specs=[lag(width)] + [cur(ow) for ow in out_widths],
        out_shape=[jax.ShapeDtypeStruct((t, width), BF16)]
                  + [jax.ShapeDtypeStruct((t, ow), BF16) for ow in out_widths],
        scratch_shapes=[pltpu.VMEM((SUBLANES, width), F32), pltpu.VMEM((1, width), F32),
                        pltpu.VMEM((tc, width), F32), pltpu.VMEM((tc, width), F32)],
        compiler_params=pltpu.CompilerParams(
            dimension_semantics=("arbitrary", "arbitrary"),
            vmem_limit_bytes=_vmem_limit(vmem + (8 << 20))),
        name="inproj_rglru",
    )(x2, w_bf16, conv_w, conv_b.reshape(1, width), w_gates_bf16, b_a.reshape(1, width),
      b_x.reshape(1, width), lam.reshape(1, width))
    return outs[0], outs[1:]


def _stickbreak_kernel(q_ref, k_ref, v_ref, g_ref, o_ref, ntri_ref, acc_ref, *, tq, nh):
    dh = SB_HEAD_DIM
    i = pl.program_id(2)

    @pl.when((pl.program_id(0) == 0) & (pl.program_id(1) == 0) & (i == 0))
    def _():
        r = lax.broadcasted_iota(jnp.int32, (tq, tq), 0)
        c = lax.broadcasted_iota(jnp.int32, (tq, tq), 1)
        ntri_ref[...] = jnp.where(r > c, -1.0, 0.0).astype(BF16)

    ntri = ntri_ref[...]

    def sweep(blocks, carries, first):
        chains = [(b, h) for b in range(len(blocks)) for h in range(nh)]
        starts = [pl.multiple_of(j * tq, tq) for j, _ in blocks]
        carries = list(carries)
        zs = []
        for b, h in chains:
            kb = k_ref[pl.ds(starts[b], tq), h * dh:(h + 1) * dh]
            zs.append(lax.dot_general(q_ref[:, h * dh:(h + 1) * dh], kb, (((1,), (1,)), ((), ())),
                                      preferred_element_type=F32))
        logits = []
        for (b, h), z in zip(chains, zs):
            mask = blocks[b][1]
            sp = jnp.maximum(z, 0.0) + jnp.log(1.0 + jnp.exp(-jnp.abs(z)))
            if mask is not None:
                sp = jnp.where(mask, sp, 0.0)
            suffix = jnp.dot(sp.astype(BF16), ntri, preferred_element_type=F32)
            logits.append((z - sp) + suffix + carries[h])
            carries[h] = carries[h] - jnp.sum(sp, axis=-1, keepdims=True)
        outs = [None] * nh
        for (b, h), lg in zip(chains, logits):
            mask = blocks[b][1]
            w = jnp.exp(lg)
            if mask is not None:
                w = jnp.where(mask, w, 0.0)
            vb = v_ref[pl.ds(starts[b], tq), h * dh:(h + 1) * dh]
            pv = jnp.dot(w.astype(BF16), vb, preferred_element_type=F32)
            outs[h] = pv if outs[h] is None else outs[h] + pv
        for h in range(nh):
            if first:
                acc_ref[:, h * dh:(h + 1) * dh] = outs[h]
            else:
                acc_ref[:, h * dh:(h + 1) * dh] += outs[h]
        return tuple(carries)

    def alive(carries):
        top = carries[0]
        for other in carries[1:]:
            top = jnp.maximum(top, other)
        return (jnp.max(top) > STICK_DEAD).astype(jnp.int32)

    r = lax.broadcasted_iota(jnp.int32, (tq, tq), 0)
    c = lax.broadcasted_iota(jnp.int32, (tq, tq), 1)
    causal = c < r
    has_prev = (jnp.zeros((tq, tq), jnp.int32) + i) >= 1
    carries = sweep([(i, causal), (jnp.maximum(i - 1, 0), has_prev)],
                    tuple(jnp.zeros((tq, 1), F32) for _ in range(nh)), True)

    def more(state):
        return jnp.logical_and(state[0] >= 0, state[1] > 0)

    def earlier_block(state):
        carries = sweep([(state[0], None)], state[2:], False)
        return (state[0] - 1, alive(carries)) + carries

    lax.while_loop(more, earlier_block, (i - 2, alive(carries)) + carries)
    o_ref[...] = (acc_ref[...] * _silu(g_ref[...].astype(F32))).astype(o_ref.dtype)


def _stickbreak(q, k, v, g, *, batch, tq, nh):
    t, width = q.shape
    seq = t // batch
    gw = nh * SB_HEAD_DIM
    groups = width // gw
    nq = seq // tq
    assert seq % tq == 0 and width % gw == 0
    q_blk = lambda b, h, i: (b * nq + i, h)
    kv_blk = lambda b, h, i: (b, h)
    vmem = 2 * 2 * seq * gw * 2 + 2 * nh * tq * tq * 8 * 4
    return pl.pallas_call(
        functools.partial(_stickbreak_kernel, tq=tq, nh=nh),
        grid=(batch, groups, nq),
        in_specs=[
            pl.BlockSpec((tq, gw), q_blk),
            pl.BlockSpec((seq, gw), kv_blk),
            pl.BlockSpec((seq, gw), kv_blk),
            pl.BlockSpec((tq, gw), q_blk),
        ],
        out_specs=pl.BlockSpec((tq, gw), q_blk),
        out_shape=jax.ShapeDtypeStruct((t, width), BF16),
        scratch_shapes=[
            pltpu.VMEM((tq, tq), BF16),
            pltpu.VMEM((tq, gw), F32),
        ],
        compiler_params=pltpu.CompilerParams(
            dimension_semantics=("arbitrary", "arbitrary", "arbitrary"),
            vmem_limit_bytes=_vmem_limit(vmem + (8 << 20))),
        name="stickbreak",
    )(q, k, v, g)


def _alibi_slope(h):
    return 2.0 ** (-8.0 * (h + 1) / C_HEADS)


def _swa_kernel(sink_ref, q_ref, kp_ref, kc_ref, vp_ref, vc_ref, g_ref, o_ref, *, tq, blocks_per_seq):
    nk = WINDOW + tq
    pairs = C_GROUP // 2
    first = (pl.program_id(0) % blocks_per_seq) == 0

    r = lax.broadcasted_iota(jnp.int32, (tq, nk), 0)
    c = lax.broadcasted_iota(jnp.int32, (tq, nk), 1)
    dist = r - c + WINDOW
    first_key = jnp.where(first, WINDOW, 0)
    valid = (dist >= 0) & (dist < WINDOW) & (c >= first_key)
    mdist = jnp.where(valid, dist.astype(F32), jnp.inf)

    lane = lax.broadcasted_iota(jnp.int32, (nk, LANES), 1)
    low = lane < C_HEAD_DIM

    def halves(prev_ref, cur_ref):
        full = jnp.concatenate([prev_ref[...], cur_ref[...]], axis=0).astype(F32)
        swapped = pltpu.roll(full, C_HEAD_DIM, axis=1)
        zero = jnp.zeros_like(full)
        even = [jnp.where(low, full, zero), jnp.where(low, swapped, zero)]
        odd = [jnp.where(low, zero, swapped), jnp.where(low, zero, full)]
        return [[e.astype(BF16) for e in even], [o.astype(BF16) for o in odd]]

    k_ext = halves(kp_ref, kc_ref)
    v_ext = halves(vp_ref, vc_ref)

    chains = [(kv, parity) for kv in range(C_KV_HEADS) for parity in range(2)]
    cols = [[(kv * pairs + p) * LANES for p in range(pairs)] for kv in range(C_KV_HEADS)]
    scores = []
    for kv, parity in chains:
        qs = jnp.concatenate([q_ref[:, c0:c0 + LANES] for c0 in cols[kv]], axis=0)
        scores.append(lax.dot_general(qs, k_ext[parity][kv], (((1,), (1,)), ((), ())),
                                      preferred_element_type=F32))
    outs = [None] * C_KV_HEADS
    for (kv, parity), s in zip(chains, scores):
        probs, invs = [], []
        for p in range(pairs):
            head = kv * C_GROUP + 2 * p + parity
            sink = sink_ref[head] * LOG2E
            sc = s[p * tq:(p + 1) * tq] - (_alibi_slope(head) * LOG2E) * mdist
            m = jnp.maximum(jnp.max(sc, axis=-1, keepdims=True), sink)
            pe = jnp.exp2(sc - m)
            denom = jnp.sum(pe, axis=-1, keepdims=True) + jnp.exp2(sink - m)
            probs.append(pe.astype(BF16))
            invs.append(1.0 / denom)
        o_par = jnp.dot(jnp.concatenate(probs, axis=0), v_ext[parity][kv],
                        preferred_element_type=F32)
        o_par = o_par * jnp.concatenate(invs, axis=0)
        outs[kv] = o_par if outs[kv] is None else outs[kv] + o_par
    for kv in range(C_KV_HEADS):
        for p, c0 in enumerate(cols[kv]):
            gate = _silu(g_ref[:, c0:c0 + LANES].astype(F32))
            o_ref[:, c0:c0 + LANES] = (outs[kv][p * tq:(p + 1) * tq] * gate).astype(o_ref.dtype)


def _swa(q, k, v, g, sinks, *, batch, tq):
    t, width = q.shape
    seq = t // batch
    assert seq % tq == 0 and tq % WINDOW == 0 and k.shape[1] == LANES
    blocks_per_seq = seq // tq
    ratio = tq // WINDOW
    cur = lambda i: (i, 0)
    prev = lambda i: (jnp.maximum(i * ratio - 1, 0), 0)
    return pl.pallas_call(
        functools.partial(_swa_kernel, tq=tq, blocks_per_seq=blocks_per_seq),
        grid=(t // tq,),
        in_specs=[
            pl.BlockSpec(memory_space=pltpu.SMEM),
            pl.BlockSpec((tq, width), cur),
            pl.BlockSpec((WINDOW, LANES), prev),
            pl.BlockSpec((tq, LANES), cur),
            pl.BlockSpec((WINDOW, LANES), prev),
            pl.BlockSpec((tq, LANES), cur),
            pl.BlockSpec((tq, width), cur),
        ],
        out_specs=pl.BlockSpec((tq, width), cur),
        out_shape=jax.ShapeDtypeStruct((t, width), BF16),
        compiler_params=pltpu.CompilerParams(
            dimension_semantics=("arbitrary",), vmem_limit_bytes=_vmem_limit(32 << 20)),
        name="swa",
    )(sinks, q, k, k, v, v, g)


def _odd_inproj_spec(d, w_in):
    kv_width = C_KV_HEADS * C_HEAD_DIM
    return (w_in.astype(BF16), [d, kv_width, kv_width, d], [BF16] * 4,
            [LOG2E / math.sqrt(C_HEAD_DIM), 1.0, 1.0, 1.0])


def _even_layer(x2, batch, w_in, conv_w, conv_b, w_gate_a, b_gate_a, w_gate_x, b_gate_x, lam,
                w_out, ln_g, ln_b, alpha, next_proj):
    d = x2.shape[1]
    w_gates = jnp.concatenate([w_gate_a, w_gate_x], axis=-1).astype(BF16)
    y_a, (b_q, b_k, b_v, b_g) = _inproj_rglru(
        x2, w_in.astype(BF16), conv_w, conv_b, w_gates, b_gate_a, b_gate_x, lam,
        out_widths=[d] * 4, out_scales=[1.0 / math.sqrt(SB_HEAD_DIM), 1.0, 1.0, 1.0],
        batch=batch, tc=256)
    y_b = _stickbreak(b_q, b_k, b_v, b_g, batch=batch, tq=256, nh=4)
    return _outproj_ln([y_a, y_b], w_out.astype(BF16), x2, ln_g, ln_b, alpha, tm=512,
                       name="outproj_ln_even", next_proj=next_proj)


def _odd_layer(x2, proj, batch, w_in, sinks, w_out, ln_g, ln_b, alpha):
    if proj is None:
        w_bf16, widths, dtypes, scales = _odd_inproj_spec(x2.shape[1], w_in)
        proj = _inproj(x2, w_bf16, widths=widths, dtypes=dtypes, scales=scales, tm=512,
                       name="inproj_odd")
    q, k, v, g = proj
    y = _swa(q, k, v, g, sinks, batch=batch, tq=WINDOW)
    return _outproj_ln([y], w_out.astype(BF16), x2, ln_g, ln_b, alpha, tm=512, name="outproj_ln_odd")


def kernel(x, e_w_in, e_conv_w, e_conv_b, e_w_gate_a, e_b_gate_a, e_w_gate_x, e_b_gate_x, e_lru_lambda, e_w_out, e_ln_g, e_ln_b, o_w_in, o_sinks, o_w_out, o_ln_g, o_ln_b):
    batch, seq, d = x.shape
    depth = e_w_in.shape[0] + o_w_in.shape[0]
    alpha = float((2 * depth) ** 0.25)
    x2 = x.reshape(batch * seq, d)
    proj = None
    for layer in range(depth):
        li = layer // 2
        if layer % 2 == 0:
            next_proj = _odd_inproj_spec(d, o_w_in[li]) if layer + 1 < depth else None
            res = _even_layer(x2, batch, e_w_in[li], e_conv_w[li], e_conv_b[li], e_w_gate_a[li],
                              e_b_gate_a[li], e_w_gate_x[li], e_b_gate_x[li], e_lru_lambda[li],
                              e_w_out[li], e_ln_g[li], e_ln_b[li], alpha, next_proj)
            x2, proj = res if next_proj is not None else (res, None)
        else:
            x2 = _odd_layer(x2, proj, batch, o_w_in[li], o_sinks[li], o_w_out[li], o_ln_g[li],
                            o_ln_b[li], alpha)
            proj = None
    return x2.reshape(batch, seq, d)
```

```python
import functools
import math

import jax
import jax.numpy as jnp
from jax import lax
from jax.experimental import pallas as pl
from jax.experimental.pallas import tpu as pltpu

F32 = jnp.float32
BF16 = jnp.bfloat16

LANES = 128
SUBLANES = 8
VMEM_BYTES_V7X = 64 * 1024 * 1024

LRU_BLOCK = 128
LRU_C = 8.0
CONV_WIDTH = 4
SB_HEAD_DIM = 128
C_HEADS = 16
C_KV_HEADS = 2
C_GROUP = C_HEADS // C_KV_HEADS
C_HEAD_DIM = 64
WINDOW = 128
LN_EPS = 1e-5
STICK_DEAD = -100.0
LOG2E = math.log2(math.e)


def _sigmoid(x):
    return 0.5 * jnp.tanh(0.5 * x) + 0.5


def _silu(x):
    return x * _sigmoid(x)


def _vmem_limit(nbytes):
    return int(min(nbytes, VMEM_BYTES_V7X - 8 * 1024 * 1024))


def _inproj_kernel(x_ref, w_ref, *out_refs, widths, scales):
    xb = x_ref[...].astype(BF16)
    col = 0
    for o_ref, width, scale in zip(out_refs, widths, scales):
        r = jnp.dot(xb, w_ref[:, col:col + width], preferred_element_type=F32)
        if scale != 1.0:
            r = r * scale
        o_ref[...] = r.astype(o_ref.dtype)
        col += width


def _inproj(x2, w_bf16, widths, dtypes, scales, *, tm, name):
    t, d = x2.shape
    n = w_bf16.shape[1]
    assert sum(widths) == n and t % tm == 0
    out_bytes = sum(tm * wd * jnp.dtype(dt).itemsize for wd, dt in zip(widths, dtypes))
    vmem = d * n * 2 + 2 * tm * d * 4 + 2 * out_bytes + tm * d * 2 + 3 * tm * max(widths) * 4
    return pl.pallas_call(
        functools.partial(_inproj_kernel, widths=tuple(widths), scales=tuple(scales)),
        grid=(t // tm,),
        in_specs=[
            pl.BlockSpec((tm, d), lambda i: (i, 0)),
            pl.BlockSpec((d, n), lambda i: (0, 0), pipeline_mode=pl.Buffered(1)),
        ],
        out_specs=[pl.BlockSpec((tm, wd), lambda i: (i, 0)) for wd in widths],
        out_shape=[jax.ShapeDtypeStruct((t, wd), dt) for wd, dt in zip(widths, dtypes)],
        compiler_params=pltpu.CompilerParams(
            dimension_semantics=("arbitrary",), vmem_limit_bytes=_vmem_limit(vmem + (8 << 20))),
        name=name,
    )(x2, w_bf16)


def _outproj_ln_kernel(*refs, n_y, alpha, next_widths, next_scales):
    y_refs = refs[:n_y]
    w_ref, x_ref, g_ref, b_ref = refs[n_y:n_y + 4]
    rest = refs[n_y + 4:]
    wn_ref = rest[0] if next_widths else None
    o_ref = rest[1] if next_widths else rest[0]
    acc = alpha * x_ref[...]
    row = 0
    for y_ref in y_refs:
        kdim = y_ref.shape[1]
        acc = acc + jnp.dot(y_ref[...], w_ref[row:row + kdim, :], preferred_element_type=F32)
        row += kdim
    mu = jnp.mean(acc, axis=-1, keepdims=True)
    dev = acc - mu
    var = jnp.mean(dev * dev, axis=-1, keepdims=True)
    out = dev * lax.rsqrt(var + LN_EPS) * g_ref[...] + b_ref[...]
    o_ref[...] = out
    if next_widths:
        ob = out.astype(BF16)
        col = 0
        for n_ref, width, scale in zip(rest[2:], next_widths, next_scales):
            r = jnp.dot(ob, wn_ref[:, col:col + width], preferred_element_type=F32)
            if scale != 1.0:
                r = r * scale
            n_ref[...] = r.astype(n_ref.dtype)
            col += width


def _outproj_ln(ys, w_bf16, x2, ln_g, ln_b, alpha, *, tm, name, next_proj=None):
    t, d = x2.shape
    kdim = w_bf16.shape[0]
    assert sum(y.shape[1] for y in ys) == kdim and t % tm == 0
    resident = lambda shape: pl.BlockSpec(shape, lambda i: (0, 0), pipeline_mode=pl.Buffered(1))
    rows = lambda width: pl.BlockSpec((tm, width), lambda i: (i, 0))
    vmem = kdim * d * 2 + 2 * tm * kdim * 2 + 4 * tm * d * 4 + 4 * tm * d * 4
    in_specs = [rows(y.shape[1]) for y in ys] + [resident((kdim, d)), rows(d), resident((1, d)),
                                                 resident((1, d))]
    operands = [*ys, w_bf16, x2, ln_g.reshape(1, d), ln_b.reshape(1, d)]
    out_specs = [rows(d)]
    out_shape = [jax.ShapeDtypeStruct((t, d), F32)]
    widths, scales = (), ()
    if next_proj is not None:
        w_next, widths, dtypes, scales = next_proj
        assert w_next.shape == (d, sum(widths))
        in_specs.append(resident(w_next.shape))
        operands.append(w_next)
        out_specs += [rows(wd) for wd in widths]
        out_shape += [jax.ShapeDtypeStruct((t, wd), dt) for wd, dt in zip(widths, dtypes)]
        vmem += d * sum(widths) * 2 + tm * d * 2 + 3 * tm * max(widths) * 4
        vmem += 2 * sum(tm * wd * jnp.dtype(dt).itemsize for wd, dt in zip(widths, dtypes))
    outs = pl.pallas_call(
        functools.partial(_outproj_ln_kernel, n_y=len(ys), alpha=alpha,
                          next_widths=tuple(widths), next_scales=tuple(scales)),
        grid=(t // tm,),
        in_specs=in_specs,
        out_specs=out_specs,
        out_shape=out_shape,
        compiler_params=pltpu.CompilerParams(
            dimension_semantics=("arbitrary",), vmem_limit_bytes=_vmem_limit(vmem + (8 << 20))),
        name=name,
    )(*operands)
    return outs[0] if next_proj is None else (outs[0], outs[1:])


def _rglru_block(xc, ag, gates, ba, bx, lam, h_prev, tc):
    log_a_unit = LRU_C * (jnp.minimum(lam, 0.0) - jnp.log(1.0 + jnp.exp(-jnp.abs(lam))))
    row8 = lax.broadcasted_iota(jnp.int32, (SUBLANES, LRU_BLOCK), 0)
    h = h_prev
    ys = []
    for g in range(tc // SUBLANES):
        rows = slice(g * SUBLANES, (g + 1) * SUBLANES)
        r = _sigmoid(gates[rows, :LRU_BLOCK] + ba)
        i = _sigmoid(gates[rows, LRU_BLOCK:] + bx)
        a = jnp.exp(log_a_unit * r)
        y = 1.0 - a * a
        u = jnp.where(y > 0.0, y * lax.rsqrt(y), 0.0) * (i * xc[rows])
        for d in (1, 2, 4):
            a_sh = jnp.where(row8 >= d, pltpu.roll(a, d, axis=0), 1.0)
            u_sh = jnp.where(row8 >= d, pltpu.roll(u, d, axis=0), 0.0)
            u = u + a * u_sh
            a = a * a_sh
        hg = u + a * h
        h = hg[SUBLANES - 1:SUBLANES]
        ys.append(hg * _silu(ag[rows]))
    return jnp.concatenate(ys, axis=0), h


def _conv_block(x, halo, cw, cb, tc):
    row8 = lax.broadcasted_iota(jnp.int32, (SUBLANES, LRU_BLOCK), 0)
    prev = halo
    out = []
    for g in range(tc // SUBLANES):
        cur = x[g * SUBLANES:(g + 1) * SUBLANES]
        xc = cb + cw[CONV_WIDTH - 1:CONV_WIDTH] * cur
        for d in range(1, CONV_WIDTH):
            xd = jnp.where(row8 < d, pltpu.roll(prev, d, axis=0), pltpu.roll(cur, d, axis=0))
            xc = xc + cw[CONV_WIDTH - 1 - d:CONV_WIDTH - d] * xd
        out.append(xc)
        prev = cur
    return jnp.concatenate(out, axis=0)


def _inproj_rglru_kernel(x_ref, w_ref, cw_ref, cb_ref, wg_ref, ba_ref, bx_ref, lam_ref,
                         ya_ref, *rest, width, out_widths, out_scales):
    out_refs, (halo_ref, h_ref, ax_ref, ag_ref) = rest[:-4], rest[-4:]
    tc = x_ref.shape[0]
    s = pl.program_id(1)
    nblk = width // LRU_BLOCK
    chunk = 2 * LRU_BLOCK

    @pl.when(s == 0)
    def _():
        ax_ref[...] = jnp.zeros_like(ax_ref)
        ag_ref[...] = jnp.zeros_like(ag_ref)

    @pl.when(s <= 1)
    def _():
        halo_ref[...] = jnp.zeros_like(halo_ref)
        h_ref[...] = jnp.zeros_like(h_ref)

    xb = x_ref[...].astype(BF16)
    n_out = sum(out_widths) // chunk

    def project(col):
        r = jnp.dot(xb, w_ref[:, col:col + chunk], preferred_element_type=F32)
        if col < width:
            ax_ref[:, col:col + chunk] = r
        elif col < 2 * width:
            ag_ref[:, col - width:col - width + chunk] = r
        else:
            off, idx = col - 2 * width, 0
            while off >= out_widths[idx]:
                off -= out_widths[idx]
                idx += 1
            if out_scales[idx] != 1.0:
                r = r * out_scales[idx]
            out_refs[idx][:, off:off + chunk] = r.astype(out_refs[idx].dtype)

    out_cols = [2 * width + c * chunk for c in range(n_out)]
    plan = []
    for n in range(nblk):
        if n % 2 == 1:
            own = [(n // 2) * chunk, width + (n // 2) * chunk]
        else:
            own = []
        take = (2 * width // chunk + n_out) // nblk - len(own)
        plan.append(own + out_cols[:take])
        out_cols = out_cols[take:]
    assert not out_cols

    for n in range(nblk):
        blk = slice(n * LRU_BLOCK, (n + 1) * LRU_BLOCK)
        x = ax_ref[:, blk]
        ag = ag_ref[:, blk]
        xc = _conv_block(x, halo_ref[:, blk], cw_ref[:, blk], cb_ref[:, blk], tc)
        halo_ref[:, blk] = x[tc - SUBLANES:]
        gates = jnp.dot(xc.astype(BF16), wg_ref[n], preferred_element_type=F32)
        for col in plan[n]:
            project(col)
        y, h_last = _rglru_block(xc, ag, gates, ba_ref[:, blk], bx_ref[:, blk], lam_ref[:, blk],
                                 h_ref[:, blk], tc)
        h_ref[:, blk] = h_last
        ya_ref[:, blk] = y.astype(ya_ref.dtype)


def _inproj_rglru(x2, w_bf16, conv_w, conv_b, w_gates_bf16, b_a, b_x, lam, out_widths, out_scales,
                  *, batch, tc):
    t, d = x2.shape
    seq = t // batch
    width = conv_w.shape[1]
    nt = seq // tc
    n = w_bf16.shape[1]
    assert seq % tc == 0 and n == 2 * width + sum(out_widths)
    cur = lambda wd: pl.BlockSpec((tc, wd), lambda b, s: (b * nt + jnp.minimum(s, nt - 1), 0))
    lag = lambda wd: pl.BlockSpec((tc, wd), lambda b, s: (b * nt + jnp.maximum(s - 1, 0), 0))
    resident = lambda shape: pl.BlockSpec(shape, lambda b, s: (0,) * len(shape),
                                          pipeline_mode=pl.Buffered(1))
    vmem = d * n * 2 + 2 * tc * d * 4 + 2 * tc * (width + sum(out_widths)) * 2 + 24 * tc * width * 4
    outs = pl.pallas_call(
        functools.partial(_inproj_rglru_kernel, width=width, out_widths=tuple(out_widths),
                          out_scales=tuple(out_scales)),
        grid=(batch, nt + 1),
        in_specs=[cur(d), resident((d, n)), resident(conv_w.shape), resident((1, width)),
                  resident(w_gates_bf16.shape), resident((1, width)), resident((1, width)),
                  resident((1, width))],
        out_specs=[lag(width)] + [cur(ow) for ow in out_widths],
        out_shape=[jax.ShapeDtypeStruct((t, width), BF16)]
                  + [jax.ShapeDtypeStruct((t, ow), BF16) for ow in out_widths],
        scratch_shapes=[pltpu.VMEM((SUBLANES, width), F32), pltpu.VMEM((1, width), F32),
                        pltpu.VMEM((tc, width), F32), pltpu.VMEM((tc, width), F32)],
        compiler_params=pltpu.CompilerParams(
            dimension_semantics=("arbitrary", "arbitrary"),
            vmem_limit_bytes=_vmem_limit(vmem + (8 << 20))),
        name="inproj_rglru",
    )(x2, w_bf16, conv_w, conv_b.reshape(1, width), w_gates_bf16, b_a.reshape(1, width),
      b_x.reshape(1, width), lam.reshape(1, width))
    return outs[0], outs[1:]


def _stickbreak_kernel(q_ref, k_ref, v_ref, g_ref, o_ref, ntri_ref, acc_ref, *, tq, nh):
    dh = SB_HEAD_DIM
    i = pl.program_id(2)

    @pl.when((pl.program_id(0) == 0) & (pl.program_id(1) == 0) & (i == 0))
    def _():
        r = lax.broadcasted_iota(jnp.int32, (tq, tq), 0)
        c = lax.broadcasted_iota(jnp.int32, (tq, tq), 1)
        ntri_ref[...] = jnp.where(r > c, -1.0, 0.0).astype(BF16)

    ntri = ntri_ref[...]

    def sweep(blocks, carries, first):
        chains = [(b, h) for b in range(len(blocks)) for h in range(nh)]
        starts = [pl.multiple_of(j * tq, tq) for j, _ in blocks]
        carries = list(carries)
        zs = []
        for b, h in chains:
            kb = k_ref[pl.ds(starts[b], tq), h * dh:(h + 1) * dh]
            zs.append(lax.dot_general(q_ref[:, h * dh:(h + 1) * dh], kb, (((1,), (1,)), ((), ())),
                                      preferred_element_type=F32))
        logits = []
        for (b, h), z in zip(chains, zs):
            mask = blocks[b][1]
            sp = jnp.maximum(z, 0.0) + jnp.log(1.0 + jnp.exp(-jnp.abs(z)))
            if mask is not None:
                sp = jnp.where(mask, sp, 0.0)
            suffix = jnp.dot(sp.astype(BF16), ntri, preferred_element_type=F32)
            logits.append((z - sp) + suffix + carries[h])
            carries[h] = carries[h] - jnp.sum(sp, axis=-1, keepdims=True)
        outs = [None] * nh
        for (b, h), lg in zip(chains, logits):
            mask = blocks[b][1]
            w = jnp.exp(lg)
            if mask is not None:
                w = jnp.where(mask, w, 0.0)
            vb = v_ref[pl.ds(starts[b], tq), h * dh:(h + 1) * dh]
            pv = jnp.dot(w.astype(BF16), vb, preferred_element_type=F32)
            outs[h] = pv if outs[h] is None else outs[h] + pv
        for h in range(nh):
            if first:
                acc_ref[:, h * dh:(h + 1) * dh] = outs[h]
            else:
                acc_ref[:, h * dh:(h + 1) * dh] += outs[h]
        return tuple(carries)

    def alive(carries):
        top = carries[0]
        for other in carries[1:]:
            top = jnp.maximum(top, other)
        return (jnp.max(top) > STICK_DEAD).astype(jnp.int32)

    r = lax.broadcasted_iota(jnp.int32, (tq, tq), 0)
    c = lax.broadcasted_iota(jnp.int32, (tq, tq), 1)
    causal = c < r
    has_prev = (jnp.zeros((tq, tq), jnp.int32) + i) >= 1
    carries = sweep([(i, causal), (jnp.maximum(i - 1, 0), has_prev)],
                    tuple(jnp.zeros((tq, 1), F32) for _ in range(nh)), True)

    def more(state):
        return jnp.logical_and(state[0] >= 0, state[1] > 0)

    def earlier_block(state):
        carries = sweep([(state[0], None)], state[2:], False)
        return (state[0] - 1, alive(carries)) + carries

    lax.while_loop(more, earlier_block, (i - 2, alive(carries)) + carries)
    o_ref[...] = (acc_ref[...] * _silu(g_ref[...].astype(F32))).astype(o_ref.dtype)


def _stickbreak(q, k, v, g, *, batch, tq, nh):
    t, width = q.shape
    seq = t // batch
    gw = nh * SB_HEAD_DIM
    groups = width // gw
    nq = seq // tq
    assert seq % tq == 0 and width % gw == 0
    q_blk = lambda b, h, i: (b * nq + i, h)
    kv_blk = lambda b, h, i: (b, h)
    vmem = 2 * 2 * seq * gw * 2 + 2 * nh * tq * tq * 8 * 4
    return pl.pallas_call(
        functools.partial(_stickbreak_kernel, tq=tq, nh=nh),
        grid=(batch, groups, nq),
        in_specs=[
            pl.BlockSpec((tq, gw), q_blk),
            pl.BlockSpec((seq, gw), kv_blk),
            pl.BlockSpec((seq, gw), kv_blk),
            pl.BlockSpec((tq, gw), q_blk),
        ],
        out_specs=pl.BlockSpec((tq, gw), q_blk),
        out_shape=jax.ShapeDtypeStruct((t, width), BF16),
        scratch_shapes=[
            pltpu.VMEM((tq, tq), BF16),
            pltpu.VMEM((tq, gw), F32),
        ],
        compiler_params=pltpu.CompilerParams(
            dimension_semantics=("arbitrary", "arbitrary", "arbitrary"),
            vmem_limit_bytes=_vmem_limit(vmem + (8 << 20))),
        name="stickbreak",
    )(q, k, v, g)


def _alibi_slope(h):
    return 2.0 ** (-8.0 * (h + 1) / C_HEADS)


def _swa_kernel(sink_ref, q_ref, kp_ref, kc_ref, vp_ref, vc_ref, g_ref, o_ref, *, tq, blocks_per_seq):
    nk = WINDOW + tq
    pairs = C_GROUP // 2
    first = (pl.program_id(0) % blocks_per_seq) == 0

    r = lax.broadcasted_iota(jnp.int32, (tq, nk), 0)
    c = lax.broadcasted_iota(jnp.int32, (tq, nk), 1)
    dist = r - c + WINDOW
    first_key = jnp.where(first, WINDOW, 0)
    valid = (dist >= 0) & (dist < WINDOW) & (c >= first_key)
    mdist = jnp.where(valid, dist.astype(F32), jnp.inf)

    lane = lax.broadcasted_iota(jnp.int32, (nk, LANES), 1)
    low = lane < C_HEAD_DIM

    def halves(prev_ref, cur_ref):
        full = jnp.concatenate([prev_ref[...], cur_ref[...]], axis=0).astype(F32)
        swapped = pltpu.roll(full, C_HEAD_DIM, axis=1)
        zero = jnp.zeros_like(full)
        even = [jnp.where(low, full, zero), jnp.where(low, swapped, zero)]
        odd = [jnp.where(low, zero, swapped), jnp.where(low, zero, full)]
        return [[e.astype(BF16) for e in even], [o.astype(BF16) for o in odd]]

    k_ext = halves(kp_ref, kc_ref)
    v_ext = halves(vp_ref, vc_ref)

    chains = [(kv, parity) for kv in range(C_KV_HEADS) for parity in range(2)]
    cols = [[(kv * pairs + p) * LANES for p in range(pairs)] for kv in range(C_KV_HEADS)]
    scores = []
    for kv, parity in chains:
        qs = jnp.concatenate([q_ref[:, c0:c0 + LANES] for c0 in cols[kv]], axis=0)
        scores.append(lax.dot_general(qs, k_ext[parity][kv], (((1,), (1,)), ((), ())),
                                      preferred_element_type=F32))
    outs = [None] * C_KV_HEADS
    for (kv, parity), s in zip(chains, scores):
        probs, invs = [], []
        for p in range(pairs):
            head = kv * C_GROUP + 2 * p + parity
            sink = sink_ref[head] * LOG2E
            sc = s[p * tq:(p + 1) * tq] - (_alibi_slope(head) * LOG2E) * mdist
            m = jnp.maximum(jnp.max(sc, axis=-1, keepdims=True), sink)
            pe = jnp.exp2(sc - m)
            denom = jnp.sum(pe, axis=-1, keepdims=True) + jnp.exp2(sink - m)
            probs.append(pe.astype(BF16))
            invs.append(1.0 / denom)
        o_par = jnp.dot(jnp.concatenate(probs, axis=0), v_ext[parity][kv],
                        preferred_element_type=F32)
        o_par = o_par * jnp.concatenate(invs, axis=0)
        outs[kv] = o_par if outs[kv] is None else outs[kv] + o_par
    for kv in range(C_KV_HEADS):
        for p, c0 in enumerate(cols[kv]):
            gate = _silu(g_ref[:, c0:c0 + LANES].astype(F32))
            o_ref[:, c0:c0 + LANES] = (outs[kv][p * tq:(p + 1) * tq] * gate).astype(o_ref.dtype)


def _swa(q, k, v, g, sinks, *, batch, tq):
    t, width = q.shape
    seq = t // batch
    assert seq % tq == 0 and tq % WINDOW == 0 and k.shape[1] == LANES
    blocks_per_seq = seq // tq
    ratio = tq // WINDOW
    cur = lambda i: (i, 0)
    prev = lambda i: (jnp.maximum(i * ratio - 1, 0), 0)
    return pl.pallas_call(
        functools.partial(_swa_kernel, tq=tq, blocks_per_seq=blocks_per_seq),
        grid=(t // tq,),
        in_specs=[
            pl.BlockSpec(memory_space=pltpu.SMEM),
            pl.BlockSpec((tq, width), cur),
            pl.BlockSpec((WINDOW, LANES), prev),
            pl.BlockSpec((tq, LANES), cur),
            pl.BlockSpec((WINDOW, LANES), prev),
            pl.BlockSpec((tq, LANES), cur),
            pl.BlockSpec((tq, width), cur),
        ],
        out_specs=pl.BlockSpec((tq, width), cur),
        out_shape=jax.ShapeDtypeStruct((t, width), BF16),
        compiler_params=pltpu.CompilerParams(
            dimension_semantics=("arbitrary",), vmem_limit_bytes=_vmem_limit(32 << 20)),
        name="swa",
    )(sinks, q, k, k, v, v, g)


def _odd_inproj_spec(d, w_in):
    kv_width = C_KV_HEADS * C_HEAD_DIM
    return (w_in.astype(BF16), [d, kv_width, kv_width, d], [BF16] * 4,
            [LOG2E / math.sqrt(C_HEAD_DIM), 1.0, 1.0, 1.0])


def _even_layer(x2, batch, w_in, conv_w, conv_b, w_gate_a, b_gate_a, w_gate_x, b_gate_x, lam,
                w_out, ln_g, ln_b, alpha, next_proj):
    d = x2.shape[1]
    w_gates = jnp.concatenate([w_gate_a, w_gate_x], axis=-1).astype(BF16)
    y_a, (b_q, b_k, b_v, b_g) = _inproj_rglru(
        x2, w_in.astype(BF16), conv_w, conv_b, w_gates, b_gate_a, b_gate_x, lam,
        out_widths=[d] * 4, out_scales=[1.0 / math.sqrt(SB_HEAD_DIM), 1.0, 1.0, 1.0],
        batch=batch, tc=256)
    y_b = _stickbreak(b_q, b_k, b_v, b_g, batch=batch, tq=256, nh=4)
    return _outproj_ln([y_a, y_b], w_out.astype(BF16), x2, ln_g, ln_b, alpha, tm=512,
                       name="outproj_ln_even", next_proj=next_proj)


def _odd_layer(x2, proj, batch, w_in, sinks, w_out, ln_g, ln_b, alpha):
    if proj is None:
        w_bf16, widths, dtypes, scales = _odd_inproj_spec(x2.shape[1], w_in)
        proj = _inproj(x2, w_bf16, widths=widths, dtypes=dtypes, scales=scales, tm=512,
                       name="inproj_odd")
    q, k, v, g = proj
    y = _swa(q, k, v, g, sinks, batch=batch, tq=WINDOW)
    return _outproj_ln([y], w_out.astype(BF16), x2, ln_g, ln_b, alpha, tm=512, name="outproj_ln_odd")


def kernel(x, e_w_in, e_conv_w, e_conv_b, e_w_gate_a, e_b_gate_a, e_w_gate_x, e_b_gate_x, e_lru_lambda, e_w_out, e_ln_g, e_ln_b, o_w_in, o_sinks, o_w_out, o_ln_g, o_ln_b):
    batch, seq, d = x.shape
    depth = e_w_in.shape[0] + o_w_in.shape[0]
    alpha = float((2 * depth) ** 0.25)
    x2 = x.reshape(batch * seq, d)
    proj = None
    for layer in range(depth):
        li = layer // 2
        if layer % 2 == 0:
            next_proj = _odd_inproj_spec(d, o_w_in[li]) if layer + 1 < depth else None
            res = _even_layer(x2, batch, e_w_in[li], e_conv_w[li], e_conv_b[li], e_w_gate_a[li],
                              e_b_gate_a[li], e_w_gate_x[li], e_b_gate_x[li], e_lru_lambda[li],
                              e_w_out[li], e_ln_g[li], e_ln_b[li], alpha, next_proj)
            x2, proj = res if next_proj is not None else (res, None)
        else:
            x2 = _odd_layer(x2, proj, batch, o_w_in[li], o_sinks[li], o_w_out[li], o_ln_g[li],
                            o_ln_b[li], alpha)
            proj = None
    return x2.reshape(batch, seq, d)
```

```python
import functools
import math

import jax
import jax.numpy as jnp
from jax import lax
from jax.experimental import pallas as pl
from jax.experimental.pallas import tpu as pltpu

F32 = jnp.float32
BF16 = jnp.bfloat16

LANES = 128
SUBLANES = 8
VMEM_BYTES_V7X = 64 * 1024 * 1024

LRU_BLOCK = 128
LRU_C = 8.0
CONV_WIDTH = 4
SB_HEAD_DIM = 128
C_HEADS = 16
C_KV_HEADS = 2
C_GROUP = C_HEADS // C_KV_HEADS
C_HEAD_DIM = 64
WINDOW = 128
LN_EPS = 1e-5
STICK_DEAD = -100.0
LOG2E = math.log2(math.e)


def _sigmoid(x):
    return 0.5 * jnp.tanh(0.5 * x) + 0.5


def _silu(x):
    return x * _sigmoid(x)


def _vmem_limit(nbytes):
    return int(min(nbytes, VMEM_BYTES_V7X - 8 * 1024 * 1024))


def _inproj_kernel(x_ref, w_ref, *out_refs, widths, scales):
    xb = x_ref[...].astype(BF16)
    col = 0
    for o_ref, width, scale in zip(out_refs, widths, scales):
        r = jnp.dot(xb, w_ref[:, col:col + width], preferred_element_type=F32)
        if scale != 1.0:
            r = r * scale
        o_ref[...] = r.astype(o_ref.dtype)
        col += width


def _inproj(x2, w_bf16, widths, dtypes, scales, *, tm, name):
    t, d = x2.shape
    n = w_bf16.shape[1]
    assert sum(widths) == n and t % tm == 0
    out_bytes = sum(tm * wd * jnp.dtype(dt).itemsize for wd, dt in zip(widths, dtypes))
    vmem = d * n * 2 + 2 * tm * d * 4 + 2 * out_bytes + tm * d * 2 + 3 * tm * max(widths) * 4
    return pl.pallas_call(
        functools.partial(_inproj_kernel, widths=tuple(widths), scales=tuple(scales)),
        grid=(t // tm,),
        in_specs=[
            pl.BlockSpec((tm, d), lambda i: (i, 0)),
            pl.BlockSpec((d, n), lambda i: (0, 0), pipeline_mode=pl.Buffered(1)),
        ],
        out_specs=[pl.BlockSpec((tm, wd), lambda i: (i, 0)) for wd in widths],
        out_shape=[jax.ShapeDtypeStruct((t, wd), dt) for wd, dt in zip(widths, dtypes)],
        compiler_params=pltpu.CompilerParams(
            dimension_semantics=("arbitrary",), vmem_limit_bytes=_vmem_limit(vmem + (8 << 20))),
        name=name,
    )(x2, w_bf16)


def _outproj_ln_kernel(*refs, n_y, alpha, next_widths, next_scales):
    y_refs = refs[:n_y]
    w_ref, x_ref, g_ref, b_ref = refs[n_y:n_y + 4]
    rest = refs[n_y + 4:]
    wn_ref = rest[0] if next_widths else None
    o_ref = rest[1] if next_widths else rest[0]
    acc = alpha * x_ref[...]
    row = 0
    for y_ref in y_refs:
        kdim = y_ref.shape[1]
        acc = acc + jnp.dot(y_ref[...], w_ref[row:row + kdim, :], preferred_element_type=F32)
        row += kdim
    mu = jnp.mean(acc, axis=-1, keepdims=True)
    dev = acc - mu
    var = jnp.mean(dev * dev, axis=-1, keepdims=True)
    out = dev * lax.rsqrt(var + LN_EPS) * g_ref[...] + b_ref[...]
    o_ref[...] = out
    if next_widths:
        ob = out.astype(BF16)
        col = 0
        for n_ref, width, scale in zip(rest[2:], next_widths, next_scales):
            r = jnp.dot(ob, wn_ref[:, col:col + width], preferred_element_type=F32)
            if scale != 1.0:
                r = r * scale
            n_ref[...] = r.astype(n_ref.dtype)
            col += width


def _outproj_ln(ys, w_bf16, x2, ln_g, ln_b, alpha, *, tm, name, next_proj=None):
    t, d = x2.shape
    kdim = w_bf16.shape[0]
    assert sum(y.shape[1] for y in ys) == kdim and t % tm == 0
    resident = lambda shape: pl.BlockSpec(shape, lambda i: (0, 0), pipeline_mode=pl.Buffered(1))
    rows = lambda width: pl.BlockSpec((tm, width), lambda i: (i, 0))
    vmem = kdim * d * 2 + 2 * tm * kdim * 2 + 4 * tm * d * 4 + 4 * tm * d * 4
    in_specs = [rows(y.shape[1]) for y in ys] + [resident((kdim, d)), rows(d), resident((1, d)),
                                                 resident((1, d))]
    operands = [*ys, w_bf16, x2, ln_g.reshape(1, d), ln_b.reshape(1, d)]
    out_specs = [rows(d)]
    out_shape = [jax.ShapeDtypeStruct((t, d), F32)]
    widths, scales = (), ()
    if next_proj is not None:
        w_next, widths, dtypes, scales = next_proj
        assert w_next.shape == (d, sum(widths))
        in_specs.append(resident(w_next.shape))
        operands.append(w_next)
        out_specs += [rows(wd) for wd in widths]
        out_shape += [jax.ShapeDtypeStruct((t, wd), dt) for wd, dt in zip(widths, dtypes)]
        vmem += d * sum(widths) * 2 + tm * d * 2 + 3 * tm * max(widths) * 4
        vmem += 2 * sum(tm * wd * jnp.dtype(dt).itemsize for wd, dt in zip(widths, dtypes))
    outs = pl.pallas_call(
        functools.partial(_outproj_ln_kernel, n_y=len(ys), alpha=alpha,
                          next_widths=tuple(widths), next_scales=tuple(scales)),
        grid=(t // tm,),
        in_specs=in_specs,
        out_specs=out_specs,
        out_shape=out_shape,
        compiler_params=pltpu.CompilerParams(
            dimension_semantics=("arbitrary",), vmem_limit_bytes=_vmem_limit(vmem + (8 << 20))),
        name=name,
    )(*operands)
    return outs[0] if next_proj is None else (outs[0], outs[1:])


def _rglru_block(xc, ag, gates, ba, bx, lam, h_prev, tc):
    log_a_unit = LRU_C * (jnp.minimum(lam, 0.0) - jnp.log(1.0 + jnp.exp(-jnp.abs(lam))))
    row8 = lax.broadcasted_iota(jnp.int32, (SUBLANES, LRU_BLOCK), 0)
    h = h_prev
    ys = []
    for g in range(tc // SUBLANES):
        rows = slice(g * SUBLANES, (g + 1) * SUBLANES)
        r = _sigmoid(gates[rows, :LRU_BLOCK] + ba)
        i = _sigmoid(gates[rows, LRU_BLOCK:] + bx)
        a = jnp.exp(log_a_unit * r)
        y = 1.0 - a * a
        u = jnp.where(y > 0.0, y * lax.rsqrt(y), 0.0) * (i * xc[rows])
        for d in (1, 2, 4):
            a_sh = jnp.where(row8 >= d, pltpu.roll(a, d, axis=0), 1.0)
            u_sh = jnp.where(row8 >= d, pltpu.roll(u, d, axis=0), 0.0)
            u = u + a * u_sh
            a = a * a_sh
        hg = u + a * h
        h = hg[SUBLANES - 1:SUBLANES]
        ys.append(hg * _silu(ag[rows]))
    return jnp.concatenate(ys, axis=0), h


def _conv_block(x, halo, cw, cb, tc):
    row8 = lax.broadcasted_iota(jnp.int32, (SUBLANES, LRU_BLOCK), 0)
    prev = halo
    out = []
    for g in range(tc // SUBLANES):
        cur = x[g * SUBLANES:(g + 1) * SUBLANES]
        xc = cb + cw[CONV_WIDTH - 1:CONV_WIDTH] * cur
        for d in range(1, CONV_WIDTH):
            xd = jnp.where(row8 < d, pltpu.roll(prev, d, axis=0), pltpu.roll(cur, d, axis=0))
            xc = xc + cw[CONV_WIDTH - 1 - d:CONV_WIDTH - d] * xd
        out.append(xc)
        prev = cur
    return jnp.concatenate(out, axis=0)


def _inproj_rglru_kernel(x_ref, w_ref, cw_ref, cb_ref, wg_ref, ba_ref, bx_ref, lam_ref,
                         ya_ref, *rest, width, out_widths, out_scales):
    out_refs, (halo_ref, h_ref, ax_ref, ag_ref) = rest[:-4], rest[-4:]
    tc = x_ref.shape[0]
    s = pl.program_id(1)
    nblk = width // LRU_BLOCK
    chunk = 2 * LRU_BLOCK

    @pl.when(s == 0)
    def _():
        ax_ref[...] = jnp.zeros_like(ax_ref)
        ag_ref[...] = jnp.zeros_like(ag_ref)

    @pl.when(s <= 1)
    def _():
        halo_ref[...] = jnp.zeros_like(halo_ref)
        h_ref[...] = jnp.zeros_like(h_ref)

    xb = x_ref[...].astype(BF16)
    n_out = sum(out_widths) // chunk

    def project(col):
        r = jnp.dot(xb, w_ref[:, col:col + chunk], preferred_element_type=F32)
        if col < width:
            ax_ref[:, col:col + chunk] = r
        elif col < 2 * width:
            ag_ref[:, col - width:col - width + chunk] = r
        else:
            off, idx = col - 2 * width, 0
            while off >= out_widths[idx]:
                off -= out_widths[idx]
                idx += 1
            if out_scales[idx] != 1.0:
                r = r * out_scales[idx]
            out_refs[idx][:, off:off + chunk] = r.astype(out_refs[idx].dtype)

    out_cols = [2 * width + c * chunk for c in range(n_out)]
    plan = []
    for n in range(nblk):
        if n % 2 == 1:
            own = [(n // 2) * chunk, width + (n // 2) * chunk]
        else:
            own = []
        take = (2 * width // chunk + n_out) // nblk - len(own)
        plan.append(own + out_cols[:take])
        out_cols = out_cols[take:]
    assert not out_cols

    for n in range(nblk):
        blk = slice(n * LRU_BLOCK, (n + 1) * LRU_BLOCK)
        x = ax_ref[:, blk]
        ag = ag_ref[:, blk]
        xc = _conv_block(x, halo_ref[:, blk], cw_ref[:, blk], cb_ref[:, blk], tc)
        halo_ref[:, blk] = x[tc - SUBLANES:]
        gates = jnp.dot(xc.astype(BF16), wg_ref[n], preferred_element_type=F32)
        for col in plan[n]:
            project(col)
        y, h_last = _rglru_block(xc, ag, gates, ba_ref[:, blk], bx_ref[:, blk], lam_ref[:, blk],
                                 h_ref[:, blk], tc)
        h_ref[:, blk] = h_last
        ya_ref[:, blk] = y.astype(ya_ref.dtype)


def _inproj_rglru(x2, w_bf16, conv_w, conv_b, w_gates_bf16, b_a, b_x, lam, out_widths, out_scales,
                  *, batch, tc):
    t, d = x2.shape
    seq = t // batch
    width = conv_w.shape[1]
    nt = seq // tc
    n = w_bf16.shape[1]
    assert seq % tc == 0 and n == 2 * width + sum(out_widths)
    cur = lambda wd: pl.BlockSpec((tc, wd), lambda b, s: (b * nt + jnp.minimum(s, nt - 1), 0))
    lag = lambda wd: pl.BlockSpec((tc, wd), lambda b, s: (b * nt + jnp.maximum(s - 1, 0), 0))
    resident = lambda shape: pl.BlockSpec(shape, lambda b, s: (0,) * len(shape),
                                          pipeline_mode=pl.Buffered(1))
    vmem = d * n * 2 + 2 * tc * d * 4 + 2 * tc * (width + sum(out_widths)) * 2 + 24 * tc * width * 4
    outs = pl.pallas_call(
        functools.partial(_inproj_rglru_kernel, width=width, out_widths=tuple(out_widths),
                          out_scales=tuple(out_scales)),
        grid=(batch, nt + 1),
        in_specs=[cur(d), resident((d, n)), resident(conv_w.shape), resident((1, width)),
                  resident(w_gates_bf16.shape), resident((1, width)), resident((1, width)),
                  resident((1, width))],
        out_specs=[lag(width)] + [cur(ow) for ow in out_widths],
        out_shape=[jax.ShapeDtypeStruct((t, width), BF16)]
                  + [jax.ShapeDtypeStruct((t, ow), BF16) for ow in out_widths],
        scratch_shapes=[pltpu.VMEM((SUBLANES, width), F32), pltpu.VMEM((1, width), F32),
                        pltpu.VMEM((tc, width), F32), pltpu.VMEM((tc, width), F32)],
        compiler_params=pltpu.CompilerParams(
            dimension_semantics=("arbitrary", "arbitrary"),
            vmem_limit_bytes=_vmem_limit(vmem + (8 << 20))),
        name="inproj_rglru",
    )(x2, w_bf16, conv_w, conv_b.reshape(1, width), w_gates_bf16, b_a.reshape(1, width),
      b_x.reshape(1, width), lam.reshape(1, width))
    return outs[0], outs[1:]


def _stickbreak_kernel(q_ref, k_ref, v_ref, g_ref, o_ref, ntri_ref, acc_ref, *, tq, nh):
    dh = SB_HEAD_DIM
    i = pl.program_id(2)

    @pl.when((pl.program_id(0) == 0) & (pl.program_id(1) == 0) & (i == 0))
    def _():
        r = lax.broadcasted_iota(jnp.int32, (tq, tq), 0)
        c = lax.broadcasted_iota(jnp.int32, (tq, tq), 1)
        ntri_ref[...] = jnp.where(r > c, -1.0, 0.0).astype(BF16)

    ntri = ntri_ref[...]

    def sweep(blocks, carries, first):
        chains = [(b, h) for b in range(len(blocks)) for h in range(nh)]
        starts = [pl.multiple_of(j * tq, tq) for j, _ in blocks]
        carries = list(carries)
        zs = []
        for b, h in chains:
            kb = k_ref[pl.ds(starts[b], tq), h * dh:(h + 1) * dh]
            zs.append(lax.dot_general(q_ref[:, h * dh:(h + 1) * dh], kb, (((1,), (1,)), ((), ())),
                                      preferred_element_type=F32))
        logits = []
        for (b, h), z in zip(chains, zs):
            mask = blocks[b][1]
            sp = jnp.maximum(z, 0.0) + jnp.log(1.0 + jnp.exp(-jnp.abs(z)))
            if mask is not None:
                sp = jnp.where(mask, sp, 0.0)
            suffix = jnp.dot(sp.astype(BF16), ntri, preferred_element_type=F32)
            logits.append((z - sp) + suffix + carries[h])
            carries[h] = carries[h] - jnp.sum(sp, axis=-1, keepdims=True)
        outs = [None] * nh
        for (b, h), lg in zip(chains, logits):
            mask = blocks[b][1]
            w = jnp.exp(lg)
            if mask is not None:
                w = jnp.where(mask, w, 0.0)
            vb = v_ref[pl.ds(starts[b], tq), h * dh:(h + 1) * dh]
            pv = jnp.dot(w.astype(BF16), vb, preferred_element_type=F32)
            outs[h] = pv if outs[h] is None else outs[h] + pv
        for h in range(nh):
            if first:
                acc_ref[:, h * dh:(h + 1) * dh] = outs[h]
            else:
                acc_ref[:, h * dh:(h + 1) * dh] += outs[h]
        return tuple(carries)

    def alive(carries):
        top = carries[0]
        for other in carries[1:]:
            top = jnp.maximum(top, other)
        return (jnp.max(top) > STICK_DEAD).astype(jnp.int32)

    r = lax.broadcasted_iota(jnp.int32, (tq, tq), 0)
    c = lax.broadcasted_iota(jnp.int32, (tq, tq), 1)
    causal = c < r
    has_prev = (jnp.zeros((tq, tq), jnp.int32) + i) >= 1
    carries = sweep([(i, causal), (jnp.maximum(i - 1, 0), has_prev)],
                    tuple(jnp.zeros((tq, 1), F32) for _ in range(nh)), True)

    def more(state):
        return jnp.logical_and(state[0] >= 0, state[1] > 0)

    def earlier_block(state):
        carries = sweep([(state[0], None)], state[2:], False)
        return (state[0] - 1, alive(carries)) + carries

    lax.while_loop(more, earlier_block, (i - 2, alive(carries)) + carries)
    o_ref[...] = (acc_ref[...] * _silu(g_ref[...].astype(F32))).astype(o_ref.dtype)


def _stickbreak(q, k, v, g, *, batch, tq, nh):
    t, width = q.shape
    seq = t // batch
    gw = nh * SB_HEAD_DIM
    groups = width // gw
    nq = seq // tq
    assert seq % tq == 0 and width % gw == 0
    q_blk = lambda b, h, i: (b * nq + i, h)
    kv_blk = lambda b, h, i: (b, h)
    vmem = 2 * 2 * seq * gw * 2 + 2 * nh * tq * tq * 8 * 4
    return pl.pallas_call(
        functools.partial(_stickbreak_kernel, tq=tq, nh=nh),
        grid=(batch, groups, nq),
        in_specs=[
            pl.BlockSpec((tq, gw), q_blk),
            pl.BlockSpec((seq, gw), kv_blk),
            pl.BlockSpec((seq, gw), kv_blk),
            pl.BlockSpec((tq, gw), q_blk),
        ],
        out_specs=pl.BlockSpec((tq, gw), q_blk),
        out_shape=jax.ShapeDtypeStruct((t, width), BF16),
        scratch_shapes=[
            pltpu.VMEM((tq, tq), BF16),
            pltpu.VMEM((tq, gw), F32),
        ],
        compiler_params=pltpu.CompilerParams(
            dimension_semantics=("arbitrary", "arbitrary", "arbitrary"),
            vmem_limit_bytes=_vmem_limit(vmem + (8 << 20))),
        name="stickbreak",
    )(q, k, v, g)


def _alibi_slope(h):
    return 2.0 ** (-8.0 * (h + 1) / C_HEADS)


def _swa_kernel(sink_ref, q_ref, kp_ref, kc_ref, vp_ref, vc_ref, g_ref, o_ref, *, tq, blocks_per_seq):
    assert tq == WINDOW
    nk = WINDOW + tq
    pairs = C_GROUP // 2
    first = (pl.program_id(0) % blocks_per_seq) == 0

    r = lax.broadcasted_iota(jnp.int32, (tq, WINDOW), 0)
    c = lax.broadcasted_iota(jnp.int32, (tq, WINDOW), 1)
    from_prev = c > r
    dist = jnp.where(from_prev, r - c + WINDOW, r - c)
    no_prev = jnp.where(first, WINDOW, -1)
    mdist = jnp.where(from_prev & (c <= no_prev), jnp.inf, dist.astype(F32))

    lane = lax.broadcasted_iota(jnp.int32, (nk, LANES), 1)
    low = lane < C_HEAD_DIM

    def halves(prev_ref, cur_ref):
        full = jnp.concatenate([prev_ref[...], cur_ref[...]], axis=0).astype(F32)
        swapped = pltpu.roll(full, C_HEAD_DIM, axis=1)
        zero = jnp.zeros_like(full)
        even = [jnp.where(low, full, zero), jnp.where(low, swapped, zero)]
        odd = [jnp.where(low, zero, swapped), jnp.where(low, zero, full)]
        return [[e.astype(BF16) for e in even], [o.astype(BF16) for o in odd]]

    k_ext = halves(kp_ref, kc_ref)
    v_ext = halves(vp_ref, vc_ref)

    chains = [(kv, parity) for kv in range(C_KV_HEADS) for parity in range(2)]
    cols = [[(kv * pairs + p) * LANES for p in range(pairs)] for kv in range(C_KV_HEADS)]
    scores = []
    for kv, parity in chains:
        qs = jnp.concatenate([q_ref[:, c0:c0 + LANES] for c0 in cols[kv]], axis=0)
        scores.append(lax.dot_general(qs, k_ext[parity][kv], (((1,), (1,)), ((), ())),
                                      preferred_element_type=F32))
    outs = [None] * C_KV_HEADS
    for (kv, parity), s in zip(chains, scores):
        probs, invs = [], []
        for p in range(pairs):
            head = kv * C_GROUP + 2 * p + parity
            sink = sink_ref[head] * LOG2E
            sh = s[p * tq:(p + 1) * tq]
            sc = jnp.where(from_prev, sh[:, :WINDOW], sh[:, WINDOW:]) - (_alibi_slope(head) * LOG2E) * mdist
            m = jnp.maximum(jnp.max(sc, axis=-1, keepdims=True), sink)
            pe = jnp.exp2(sc - m)
            denom = jnp.sum(pe, axis=-1, keepdims=True) + jnp.exp2(sink - m)
            unfolded = jnp.concatenate([jnp.where(from_prev, pe, 0.0), jnp.where(from_prev, 0.0, pe)],
                                       axis=1)
            probs.append(unfolded.astype(BF16))
            invs.append(1.0 / denom)
        o_par = jnp.dot(jnp.concatenate(probs, axis=0), v_ext[parity][kv],
                        preferred_element_type=F32)
        o_par = o_par * jnp.concatenate(invs, axis=0)
        outs[kv] = o_par if outs[kv] is None else outs[kv] + o_par
    for kv in range(C_KV_HEADS):
        for p, c0 in enumerate(cols[kv]):
            gate = _silu(g_ref[:, c0:c0 + LANES].astype(F32))
            o_ref[:, c0:c0 + LANES] = (outs[kv][p * tq:(p + 1) * tq] * gate).astype(o_ref.dtype)


def _swa(q, k, v, g, sinks, *, batch, tq):
    t, width = q.shape
    seq = t // batch
    assert seq % tq == 0 and tq % WINDOW == 0 and k.shape[1] == LANES
    blocks_per_seq = seq // tq
    ratio = tq // WINDOW
    cur = lambda i: (i, 0)
    prev = lambda i: (jnp.maximum(i * ratio - 1, 0), 0)
    return pl.pallas_call(
        functools.partial(_swa_kernel, tq=tq, blocks_per_seq=blocks_per_seq),
        grid=(t // tq,),
        in_specs=[
            pl.BlockSpec(memory_space=pltpu.SMEM),
            pl.BlockSpec((tq, width), cur),
            pl.BlockSpec((WINDOW, LANES), prev),
            pl.BlockSpec((tq, LANES), cur),
            pl.BlockSpec((WINDOW, LANES), prev),
            pl.BlockSpec((tq, LANES), cur),
            pl.BlockSpec((tq, width), cur),
        ],
        out_specs=pl.BlockSpec((tq, width), cur),
        out_shape=jax.ShapeDtypeStruct((t, width), BF16),
        compiler_params=pltpu.CompilerParams(
            dimension_semantics=("arbitrary",), vmem_limit_bytes=_vmem_limit(32 << 20)),
        name="swa",
    )(sinks, q, k, k, v, v, g)


def _odd_inproj_spec(d, w_in):
    kv_width = C_KV_HEADS * C_HEAD_DIM
    return (w_in.astype(BF16), [d, kv_width, kv_width, d], [BF16] * 4,
            [LOG2E / math.sqrt(C_HEAD_DIM), 1.0, 1.0, 1.0])


def _even_layer(x2, batch, w_in, conv_w, conv_b, w_gate_a, b_gate_a, w_gate_x, b_gate_x, lam,
                w_out, ln_g, ln_b, alpha, next_proj):
    d = x2.shape[1]
    w_gates = jnp.concatenate([w_gate_a, w_gate_x], axis=-1).astype(BF16)
    y_a, (b_q, b_k, b_v, b_g) = _inproj_rglru(
        x2, w_in.astype(BF16), conv_w, conv_b, w_gates, b_gate_a, b_gate_x, lam,
        out_widths=[d] * 4, out_scales=[1.0 / math.sqrt(SB_HEAD_DIM), 1.0, 1.0, 1.0],
        batch=batch, tc=256)
    y_b = _stickbreak(b_q, b_k, b_v, b_g, batch=batch, tq=256, nh=4)
    return _outproj_ln([y_a, y_b], w_out.astype(BF16), x2, ln_g, ln_b, alpha, tm=512,
                       name="outproj_ln_even", next_proj=next_proj)


def _odd_layer(x2, proj, batch, w_in, sinks, w_out, ln_g, ln_b, alpha):
    if proj is None:
        w_bf16, widths, dtypes, scales = _odd_inproj_spec(x2.shape[1], w_in)
        proj = _inproj(x2, w_bf16, widths=widths, dtypes=dtypes, scales=scales, tm=512,
                       name="inproj_odd")
    q, k, v, g = proj
    y = _swa(q, k, v, g, sinks, batch=batch, tq=WINDOW)
    return _outproj_ln([y], w_out.astype(BF16), x2, ln_g, ln_b, alpha, tm=512, name="outproj_ln_odd")


def kernel(x, e_w_in, e_conv_w, e_conv_b, e_w_gate_a, e_b_gate_a, e_w_gate_x, e_b_gate_x, e_lru_lambda, e_w_out, e_ln_g, e_ln_b, o_w_in, o_sinks, o_w_out, o_ln_g, o_ln_b):
    batch, seq, d = x.shape
    depth = e_w_in.shape[0] + o_w_in.shape[0]
    alpha = float((2 * depth) ** 0.25)
    x2 = x.reshape(batch * seq, d)
    proj = None
    for layer in range(depth):
        li = layer // 2
        if layer % 2 == 0:
            next_proj = _odd_inproj_spec(d, o_w_in[li]) if layer + 1 < depth else None
            res = _even_layer(x2, batch, e_w_in[li], e_conv_w[li], e_conv_b[li], e_w_gate_a[li],
                              e_b_gate_a[li], e_w_gate_x[li], e_b_gate_x[li], e_lru_lambda[li],
                              e_w_out[li], e_ln_g[li], e_ln_b[li], alpha, next_proj)
            x2, proj = res if next_proj is not None else (res, None)
        else:
            x2 = _odd_layer(x2, proj, batch, o_w_in[li], o_sinks[li], o_w_out[li], o_ln_g[li],
                            o_ln_b[li], alpha)
            proj = None
    return x2.reshape(batch, seq, d)
```

```python
import functools
import math

import jax
import jax.numpy as jnp
from jax import lax
from jax.experimental import pallas as pl
from jax.experimental.pallas import tpu as pltpu

F32 = jnp.float32
BF16 = jnp.bfloat16

LANES = 128
SUBLANES = 8
VMEM_BYTES_V7X = 64 * 1024 * 1024

LRU_BLOCK = 128
LRU_C = 8.0
CONV_WIDTH = 4
SB_HEAD_DIM = 128
C_HEADS = 16
C_KV_HEADS = 2
C_GROUP = C_HEADS // C_KV_HEADS
C_HEAD_DIM = 64
WINDOW = 128
LN_EPS = 1e-5
STICK_DEAD = -100.0
LOG2E = math.log2(math.e)


def _sigmoid(x):
    return 0.5 * jnp.tanh(0.5 * x) + 0.5


def _silu(x):
    return x * _sigmoid(x)


def _vmem_limit(nbytes):
    return int(min(nbytes, VMEM_BYTES_V7X - 8 * 1024 * 1024))


def _inproj_kernel(x_ref, w_ref, *out_refs, widths, scales):
    xb = x_ref[...].astype(BF16)
    col = 0
    for o_ref, width, scale in zip(out_refs, widths, scales):
        r = jnp.dot(xb, w_ref[:, col:col + width], preferred_element_type=F32)
        if scale != 1.0:
            r = r * scale
        o_ref[...] = r.astype(o_ref.dtype)
        col += width


def _inproj(x2, w_bf16, widths, dtypes, scales, *, tm, name):
    t, d = x2.shape
    n = w_bf16.shape[1]
    assert sum(widths) == n and t % tm == 0
    out_bytes = sum(tm * wd * jnp.dtype(dt).itemsize for wd, dt in zip(widths, dtypes))
    vmem = d * n * 2 + 2 * tm * d * 4 + 2 * out_bytes + tm * d * 2 + 3 * tm * max(widths) * 4
    return pl.pallas_call(
        functools.partial(_inproj_kernel, widths=tuple(widths), scales=tuple(scales)),
        grid=(t // tm,),
        in_specs=[
            pl.BlockSpec((tm, d), lambda i: (i, 0)),
            pl.BlockSpec((d, n), lambda i: (0, 0), pipeline_mode=pl.Buffered(1)),
        ],
        out_specs=[pl.BlockSpec((tm, wd), lambda i: (i, 0)) for wd in widths],
        out_shape=[jax.ShapeDtypeStruct((t, wd), dt) for wd, dt in zip(widths, dtypes)],
        compiler_params=pltpu.CompilerParams(
            dimension_semantics=("arbitrary",), vmem_limit_bytes=_vmem_limit(vmem + (8 << 20))),
        name=name,
    )(x2, w_bf16)


def _outproj_ln_kernel(*refs, n_y, alpha, next_widths, next_scales):
    y_refs = refs[:n_y]
    w_ref, x_ref, g_ref, b_ref = refs[n_y:n_y + 4]
    rest = refs[n_y + 4:]
    wn_ref = rest[0] if next_widths else None
    o_ref = rest[1] if next_widths else rest[0]
    acc = alpha * x_ref[...]
    row = 0
    for y_ref in y_refs:
        kdim = y_ref.shape[1]
        acc = acc + jnp.dot(y_ref[...], w_ref[row:row + kdim, :], preferred_element_type=F32)
        row += kdim
    mu = jnp.mean(acc, axis=-1, keepdims=True)
    dev = acc - mu
    var = jnp.mean(dev * dev, axis=-1, keepdims=True)
    out = dev * lax.rsqrt(var + LN_EPS) * g_ref[...] + b_ref[...]
    o_ref[...] = out
    if next_widths:
        ob = out.astype(BF16)
        col = 0
        for n_ref, width, scale in zip(rest[2:], next_widths, next_scales):
            r = jnp.dot(ob, wn_ref[:, col:col + width], preferred_element_type=F32)
            if scale != 1.0:
                r = r * scale
            n_ref[...] = r.astype(n_ref.dtype)
            col += width


def _outproj_ln(ys, w_bf16, x2, ln_g, ln_b, alpha, *, tm, name, next_proj=None):
    t, d = x2.shape
    kdim = w_bf16.shape[0]
    assert sum(y.shape[1] for y in ys) == kdim and t % tm == 0
    resident = lambda shape: pl.BlockSpec(shape, lambda i: (0, 0), pipeline_mode=pl.Buffered(1))
    rows = lambda width: pl.BlockSpec((tm, width), lambda i: (i, 0))
    vmem = kdim * d * 2 + 2 * tm * kdim * 2 + 4 * tm * d * 4 + 4 * tm * d * 4
    in_specs = [rows(y.shape[1]) for y in ys] + [resident((kdim, d)), rows(d), resident((1, d)),
                                                 resident((1, d))]
    operands = [*ys, w_bf16, x2, ln_g.reshape(1, d), ln_b.reshape(1, d)]
    out_specs = [rows(d)]
    out_shape = [jax.ShapeDtypeStruct((t, d), F32)]
    widths, scales = (), ()
    if next_proj is not None:
        w_next, widths, dtypes, scales = next_proj
        assert w_next.shape == (d, sum(widths))
        in_specs.append(resident(w_next.shape))
        operands.append(w_next)
        out_specs += [rows(wd) for wd in widths]
        out_shape += [jax.ShapeDtypeStruct((t, wd), dt) for wd, dt in zip(widths, dtypes)]
        vmem += d * sum(widths) * 2 + tm * d * 2 + 3 * tm * max(widths) * 4
        vmem += 2 * sum(tm * wd * jnp.dtype(dt).itemsize for wd, dt in zip(widths, dtypes))
    outs = pl.pallas_call(
        functools.partial(_outproj_ln_kernel, n_y=len(ys), alpha=alpha,
                          next_widths=tuple(widths), next_scales=tuple(scales)),
        grid=(t // tm,),
        in_specs=in_specs,
        out_specs=out_specs,
        out_shape=out_shape,
        compiler_params=pltpu.CompilerParams(
            dimension_semantics=("arbitrary",), vmem_limit_bytes=_vmem_limit(vmem + (8 << 20))),
        name=name,
    )(*operands)
    return outs[0] if next_proj is None else (outs[0], outs[1:])


def _rglru_block(xc, ag, gates, ba, bx, lam, h_prev, tc):
    log_a_unit = LRU_C * (jnp.minimum(lam, 0.0) - jnp.log(1.0 + jnp.exp(-jnp.abs(lam))))
    row8 = lax.broadcasted_iota(jnp.int32, (SUBLANES, LRU_BLOCK), 0)
    h = h_prev
    ys = []
    for g in range(tc // SUBLANES):
        rows = slice(g * SUBLANES, (g + 1) * SUBLANES)
        r = _sigmoid(gates[rows, :LRU_BLOCK] + ba)
        i = _sigmoid(gates[rows, LRU_BLOCK:] + bx)
        a = jnp.exp(log_a_unit * r)
        y = 1.0 - a * a
        u = jnp.where(y > 0.0, y * lax.rsqrt(y), 0.0) * (i * xc[rows])
        for d in (1, 2, 4):
            a_sh = jnp.where(row8 >= d, pltpu.roll(a, d, axis=0), 1.0)
            u_sh = jnp.where(row8 >= d, pltpu.roll(u, d, axis=0), 0.0)
            u = u + a * u_sh
            a = a * a_sh
        hg = u + a * h
        h = hg[SUBLANES - 1:SUBLANES]
        ys.append(hg * _silu(ag[rows]))
    return jnp.concatenate(ys, axis=0), h


def _conv_block(x, halo, cw, cb, tc):
    row8 = lax.broadcasted_iota(jnp.int32, (SUBLANES, LRU_BLOCK), 0)
    prev = halo
    out = []
    for g in range(tc // SUBLANES):
        cur = x[g * SUBLANES:(g + 1) * SUBLANES]
        xc = cb + cw[CONV_WIDTH - 1:CONV_WIDTH] * cur
        for d in range(1, CONV_WIDTH):
            xd = jnp.where(row8 < d, pltpu.roll(prev, d, axis=0), pltpu.roll(cur, d, axis=0))
            xc = xc + cw[CONV_WIDTH - 1 - d:CONV_WIDTH - d] * xd
        out.append(xc)
        prev = cur
    return jnp.concatenate(out, axis=0)


def _inproj_rglru_kernel(x_ref, w_ref, cw_ref, cb_ref, wg_ref, ba_ref, bx_ref, lam_ref,
                         ya_ref, *rest, width, out_widths, out_scales):
    out_refs, (halo_ref, h_ref, ax_ref, ag_ref) = rest[:-4], rest[-4:]
    tc = x_ref.shape[0]
    s = pl.program_id(1)
    nblk = width // LRU_BLOCK
    chunk = 2 * LRU_BLOCK

    @pl.when(s == 0)
    def _():
        ax_ref[...] = jnp.zeros_like(ax_ref)
        ag_ref[...] = jnp.zeros_like(ag_ref)

    @pl.when(s <= 1)
    def _():
        halo_ref[...] = jnp.zeros_like(halo_ref)
        h_ref[...] = jnp.zeros_like(h_ref)

    xb = x_ref[...].astype(BF16)
    n_out = sum(out_widths) // chunk

    def project(col):
        r = jnp.dot(xb, w_ref[:, col:col + chunk], preferred_element_type=F32)
        if col < width:
            ax_ref[:, col:col + chunk] = r
        elif col < 2 * width:
            ag_ref[:, col - width:col - width + chunk] = r
        else:
            off, idx = col - 2 * width, 0
            while off >= out_widths[idx]:
                off -= out_widths[idx]
                idx += 1
            if out_scales[idx] != 1.0:
                r = r * out_scales[idx]
            out_refs[idx][:, off:off + chunk] = r.astype(out_refs[idx].dtype)

    out_cols = [2 * width + c * chunk for c in range(n_out)]
    plan = []
    for n in range(nblk):
        if n % 2 == 1:
            own = [(n // 2) * chunk, width + (n // 2) * chunk]
        else:
            own = []
        take = (2 * width // chunk + n_out) // nblk - len(own)
        plan.append(own + out_cols[:take])
        out_cols = out_cols[take:]
    assert not out_cols

    for n in range(nblk):
        blk = slice(n * LRU_BLOCK, (n + 1) * LRU_BLOCK)
        x = ax_ref[:, blk]
        ag = ag_ref[:, blk]
        xc = _conv_block(x, halo_ref[:, blk], cw_ref[:, blk], cb_ref[:, blk], tc)
        halo_ref[:, blk] = x[tc - SUBLANES:]
        gates = jnp.dot(xc.astype(BF16), wg_ref[n], preferred_element_type=F32)
        for col in plan[n]:
            project(col)
        y, h_last = _rglru_block(xc, ag, gates, ba_ref[:, blk], bx_ref[:, blk], lam_ref[:, blk],
                                 h_ref[:, blk], tc)
        h_ref[:, blk] = h_last
        ya_ref[:, blk] = y.astype(ya_ref.dtype)


def _inproj_rglru(x2, w_bf16, conv_w, conv_b, w_gates_bf16, b_a, b_x, lam, out_widths, out_scales,
                  *, batch, tc):
    t, d = x2.shape
    seq = t // batch
    width = conv_w.shape[1]
    nt = seq // tc
    n = w_bf16.shape[1]
    assert seq % tc == 0 and n == 2 * width + sum(out_widths)
    cur = lambda wd: pl.BlockSpec((tc, wd), lambda b, s: (b * nt + jnp.minimum(s, nt - 1), 0))
    lag = lambda wd: pl.BlockSpec((tc, wd), lambda b, s: (b * nt + jnp.maximum(s - 1, 0), 0))
    resident = lambda shape: pl.BlockSpec(shape, lambda b, s: (0,) * len(shape),
                                          pipeline_mode=pl.Buffered(1))
    vmem = d * n * 2 + 2 * tc * d * 4 + 2 * tc * (width + sum(out_widths)) * 2 + 24 * tc * width * 4
    outs = pl.pallas_call(
        functools.partial(_inproj_rglru_kernel, width=width, out_widths=tuple(out_widths),
                          out_scales=tuple(out_scales)),
        grid=(batch, nt + 1),
        in_specs=[cur(d), resident((d, n)), resident(conv_w.shape), resident((1, width)),
                  resident(w_gates_bf16.shape), resident((1, width)), resident((1, width)),
                  resident((1, width))],
        out_specs=[lag(width)] + [cur(ow) for ow in out_widths],
        out_shape=[jax.ShapeDtypeStruct((t, width), BF16)]
                  + [jax.ShapeDtypeStruct((t, ow), BF16) for ow in out_widths],
        scratch_shapes=[pltpu.VMEM((SUBLANES, width), F32), pltpu.VMEM((1, width), F32),
                        pltpu.VMEM((tc, width), F32), pltpu.VMEM((tc, width), F32)],
        compiler_params=pltpu.CompilerParams(
            dimension_semantics=("arbitrary", "arbitrary"),
            vmem_limit_bytes=_vmem_limit(vmem + (8 << 20))),
        name="inproj_rglru",
    )(x2, w_bf16, conv_w, conv_b.reshape(1, width), w_gates_bf16, b_a.reshape(1, width),
      b_x.reshape(1, width), lam.reshape(1, width))
    return outs[0], outs[1:]


def _stickbreak_kernel(q_ref, k_ref, v_ref, g_ref, o_ref, ntri_ref, acc_ref, *, tq, nh):
    dh = SB_HEAD_DIM
    i = pl.program_id(2)

    @pl.when((pl.program_id(0) == 0) & (pl.program_id(1) == 0) & (i == 0))
    def _():
        r = lax.broadcasted_iota(jnp.int32, (tq, tq), 0)
        c = lax.broadcasted_iota(jnp.int32, (tq, tq), 1)
        ntri_ref[...] = jnp.where(r > c, -1.0, 0.0).astype(BF16)

    ntri = ntri_ref[...]

    def sweep(blocks, carries, first):
        chains = [(b, h) for b in range(len(blocks)) for h in range(nh)]
        starts = [pl.multiple_of(j * tq, tq) for j, _ in blocks]
        carries = list(carries)
        zs = []
        for b, h in chains:
            kb = k_ref[pl.ds(starts[b], tq), h * dh:(h + 1) * dh]
            zs.append(lax.dot_general(q_ref[:, h * dh:(h + 1) * dh], kb, (((1,), (1,)), ((), ())),
                                      preferred_element_type=F32))
        logits = []
        for (b, h), z in zip(chains, zs):
            mask = blocks[b][1]
            sp = jnp.maximum(z, 0.0) + jnp.log(1.0 + jnp.exp(-jnp.abs(z)))
            if mask is not None:
                sp = jnp.where(mask, sp, 0.0)
            suffix = jnp.dot(sp.astype(BF16), ntri, preferred_element_type=F32)
            logits.append((z - sp) + suffix + carries[h])
            carries[h] = carries[h] - jnp.sum(sp, axis=-1, keepdims=True)
        outs = [None] * nh
        for (b, h), lg in zip(chains, logits):
            mask = blocks[b][1]
            w = jnp.exp(lg)
            if mask is not None:
                w = jnp.where(mask, w, 0.0)
            vb = v_ref[pl.ds(starts[b], tq), h * dh:(h + 1) * dh]
            pv = jnp.dot(w.astype(BF16), vb, preferred_element_type=F32)
            outs[h] = pv if outs[h] is None else outs[h] + pv
        for h in range(nh):
            if first:
                acc_ref[:, h * dh:(h + 1) * dh] = outs[h]
            else:
                acc_ref[:, h * dh:(h + 1) * dh] += outs[h]
        return tuple(carries)

    def alive(carries):
        top = carries[0]
        for other in carries[1:]:
            top = jnp.maximum(top, other)
        return (jnp.max(top) > STICK_DEAD).astype(jnp.int32)

    r = lax.broadcasted_iota(jnp.int32, (tq, tq), 0)
    c = lax.broadcasted_iota(jnp.int32, (tq, tq), 1)
    causal = c < r
    has_prev = (jnp.zeros((tq, tq), jnp.int32) + i) >= 1
    carries = sweep([(i, causal), (jnp.maximum(i - 1, 0), has_prev)],
                    tuple(jnp.zeros((tq, 1), F32) for _ in range(nh)), True)

    def more(state):
        return jnp.logical_and(state[0] >= 0, state[1] > 0)

    def earlier_block(state):
        carries = sweep([(state[0], None)], state[2:], False)
        return (state[0] - 1, alive(carries)) + carries

    lax.while_loop(more, earlier_block, (i - 2, alive(carries)) + carries)
    o_ref[...] = (acc_ref[...] * _silu(g_ref[...].astype(F32))).astype(o_ref.dtype)


def _stickbreak(q, k, v, g, *, batch, tq, nh):
    t, width = q.shape
    seq = t // batch
    gw = nh * SB_HEAD_DIM
    groups = width // gw
    nq = seq // tq
    assert seq % tq == 0 and width % gw == 0
    q_blk = lambda b, h, i: (b * nq + i, h)
    kv_blk = lambda b, h, i: (b, h)
    vmem = 2 * 2 * seq * gw * 2 + 2 * nh * tq * tq * 8 * 4
    return pl.pallas_call(
        functools.partial(_stickbreak_kernel, tq=tq, nh=nh),
        grid=(batch, groups, nq),
        in_specs=[
            pl.BlockSpec((tq, gw), q_blk),
            pl.BlockSpec((seq, gw), kv_blk),
            pl.BlockSpec((seq, gw), kv_blk),
            pl.BlockSpec((tq, gw), q_blk),
        ],
        out_specs=pl.BlockSpec((tq, gw), q_blk),
        out_shape=jax.ShapeDtypeStruct((t, width), BF16),
        scratch_shapes=[
            pltpu.VMEM((tq, tq), BF16),
            pltpu.VMEM((tq, gw), F32),
        ],
        compiler_params=pltpu.CompilerParams(
            dimension_semantics=("arbitrary", "arbitrary", "arbitrary"),
            vmem_limit_bytes=_vmem_limit(vmem + (8 << 20))),
        name="stickbreak",
    )(q, k, v, g)


def _alibi_slope(h):
    return 2.0 ** (-8.0 * (h + 1) / C_HEADS)


def _swa_kernel(sink_ref, q_ref, kp_ref, kc_ref, vp_ref, vc_ref, g_ref, o_ref, *, tq, blocks_per_seq):
    assert tq == WINDOW
    nk = WINDOW + tq
    pairs = C_GROUP // 2
    first = (pl.program_id(0) % blocks_per_seq) == 0

    r = lax.broadcasted_iota(jnp.int32, (tq, WINDOW), 0)
    c = lax.broadcasted_iota(jnp.int32, (tq, WINDOW), 1)
    from_prev = c > r
    dist = jnp.where(from_prev, r - c + WINDOW, r - c)
    no_prev = jnp.where(first, WINDOW, -1)
    mdist = jnp.where(from_prev & (c <= no_prev), jnp.inf, dist.astype(F32))

    lane = lax.broadcasted_iota(jnp.int32, (nk, LANES), 1)
    low = lane < C_HEAD_DIM

    def halves(prev_ref, cur_ref):
        full = jnp.concatenate([prev_ref[...], cur_ref[...]], axis=0).astype(F32)
        swapped = pltpu.roll(full, C_HEAD_DIM, axis=1)
        zero = jnp.zeros_like(full)
        even = [jnp.where(low, full, zero), jnp.where(low, swapped, zero)]
        odd = [jnp.where(low, zero, swapped), jnp.where(low, zero, full)]
        return [[e.astype(BF16) for e in even], [o.astype(BF16) for o in odd]]

    k_ext = halves(kp_ref, kc_ref)
    v_ext = halves(vp_ref, vc_ref)

    chains = [(kv, parity) for kv in range(C_KV_HEADS) for parity in range(2)]
    cols = [[(kv * pairs + p) * LANES for p in range(pairs)] for kv in range(C_KV_HEADS)]
    scores = []
    for kv, parity in chains:
        qs = jnp.concatenate([q_ref[:, c0:c0 + LANES] for c0 in cols[kv]], axis=0)
        scores.append(lax.dot_general(qs, k_ext[parity][kv], (((1,), (1,)), ((), ())),
                                      preferred_element_type=F32))
    outs = [None] * C_KV_HEADS
    for (kv, parity), s in zip(chains, scores):
        probs, invs = [], []
        for p in range(pairs):
            head = kv * C_GROUP + 2 * p + parity
            sink = sink_ref[head] * LOG2E
            sh = s[p * tq:(p + 1) * tq]
            sc = jnp.where(from_prev, sh[:, :WINDOW], sh[:, WINDOW:]) - (_alibi_slope(head) * LOG2E) * mdist
            m = jnp.maximum(jnp.max(sc, axis=-1, keepdims=True), sink)
            pe = jnp.exp2(sc - m)
            denom = jnp.sum(pe, axis=-1, keepdims=True) + jnp.exp2(sink - m)
            unfolded = jnp.concatenate([jnp.where(from_prev, pe, 0.0), jnp.where(from_prev, 0.0, pe)],
                                       axis=1)
            probs.append(unfolded.astype(BF16))
            invs.append(1.0 / denom)
        o_par = jnp.dot(jnp.concatenate(probs, axis=0), v_ext[parity][kv],
                        preferred_element_type=F32)
        o_par = o_par * jnp.concatenate(invs, axis=0)
        outs[kv] = o_par if outs[kv] is None else outs[kv] + o_par
    for kv in range(C_KV_HEADS):
        for p, c0 in enumerate(cols[kv]):
            gate = _silu(g_ref[:, c0:c0 + LANES].astype(F32))
            o_ref[:, c0:c0 + LANES] = (outs[kv][p * tq:(p + 1) * tq] * gate).astype(o_ref.dtype)


def _swa(q, k, v, g, sinks, *, batch, tq):
    t, width = q.shape
    seq = t // batch
    assert seq % tq == 0 and tq % WINDOW == 0 and k.shape[1] == LANES
    blocks_per_seq = seq // tq
    ratio = tq // WINDOW
    cur = lambda i: (i, 0)
    prev = lambda i: (jnp.maximum(i * ratio - 1, 0), 0)
    return pl.pallas_call(
        functools.partial(_swa_kernel, tq=tq, blocks_per_seq=blocks_per_seq),
        grid=(t // tq,),
        in_specs=[
            pl.BlockSpec(memory_space=pltpu.SMEM),
            pl.BlockSpec((tq, width), cur),
            pl.BlockSpec((WINDOW, LANES), prev),
            pl.BlockSpec((tq, LANES), cur),
            pl.BlockSpec((WINDOW, LANES), prev),
            pl.BlockSpec((tq, LANES), cur),
            pl.BlockSpec((tq, width), cur),
        ],
        out_specs=pl.BlockSpec((tq, width), cur),
        out_shape=jax.ShapeDtypeStruct((t, width), BF16),
        compiler_params=pltpu.CompilerParams(
            dimension_semantics=("arbitrary",), vmem_limit_bytes=_vmem_limit(32 << 20)),
        name="swa",
    )(sinks, q, k, k, v, v, g)


def _odd_inproj_spec(d, w_in):
    kv_width = C_KV_HEADS * C_HEAD_DIM
    return (w_in.astype(BF16), [d, kv_width, kv_width, d], [BF16] * 4,
            [LOG2E / math.sqrt(C_HEAD_DIM), 1.0, 1.0, 1.0])


def _even_layer(x2, batch, w_in, conv_w, conv_b, w_gate_a, b_gate_a, w_gate_x, b_gate_x, lam,
                w_out, ln_g, ln_b, alpha, next_proj):
    d = x2.shape[1]
    w_gates = jnp.concatenate([w_gate_a, w_gate_x], axis=-1).astype(BF16)
    y_a, (b_q, b_k, b_v, b_g) = _inproj_rglru(
        x2, w_in.astype(BF16), conv_w, conv_b, w_gates, b_gate_a, b_gate_x, lam,
        out_widths=[d] * 4, out_scales=[1.0 / math.sqrt(SB_HEAD_DIM), 1.0, 1.0, 1.0],
        batch=batch, tc=256)
    y_b = _stickbreak(b_q, b_k, b_v, b_g, batch=batch, tq=256, nh=4)
    return _outproj_ln([y_a, y_b], w_out.astype(BF16), x2, ln_g, ln_b, alpha, tm=512,
                       name="outproj_ln_even", next_proj=next_proj)


def _odd_layer(x2, proj, batch, w_in, sinks, w_out, ln_g, ln_b, alpha):
    if proj is None:
        w_bf16, widths, dtypes, scales = _odd_inproj_spec(x2.shape[1], w_in)
        proj = _inproj(x2, w_bf16, widths=widths, dtypes=dtypes, scales=scales, tm=512,
                       name="inproj_odd")
    q, k, v, g = proj
    y = _swa(q, k, v, g, sinks, batch=batch, tq=WINDOW)
    return _outproj_ln([y], w_out.astype(BF16), x2, ln_g, ln_b, alpha, tm=1024, name="outproj_ln_odd")


def kernel(x, e_w_in, e_conv_w, e_conv_b, e_w_gate_a, e_b_gate_a, e_w_gate_x, e_b_gate_x, e_lru_lambda, e_w_out, e_ln_g, e_ln_b, o_w_in, o_sinks, o_w_out, o_ln_g, o_ln_b):
    batch, seq, d = x.shape
    depth = e_w_in.shape[0] + o_w_in.shape[0]
    alpha = float((2 * depth) ** 0.25)
    x2 = x.reshape(batch * seq, d)
    proj = None
    for layer in range(depth):
        li = layer // 2
        if layer % 2 == 0:
            next_proj = _odd_inproj_spec(d, o_w_in[li]) if layer + 1 < depth else None
            res = _even_layer(x2, batch, e_w_in[li], e_conv_w[li], e_conv_b[li], e_w_gate_a[li],
                              e_b_gate_a[li], e_w_gate_x[li], e_b_gate_x[li], e_lru_lambda[li],
                              e_w_out[li], e_ln_g[li], e_ln_b[li], alpha, next_proj)
            x2, proj = res if next_proj is not None else (res, None)
        else:
            x2 = _odd_layer(x2, proj, batch, o_w_in[li], o_sinks[li], o_w_out[li], o_ln_g[li],
                            o_ln_b[li], alpha)
            proj = None
    return x2.reshape(batch, seq, d)
```

```python
import functools
import math

import jax
import jax.numpy as jnp
from jax import lax
from jax.experimental import pallas as pl
from jax.experimental.pallas import tpu as pltpu

F32 = jnp.float32
BF16 = jnp.bfloat16

LANES = 128
SUBLANES = 8
VMEM_BYTES_V7X = 64 * 1024 * 1024

LRU_BLOCK = 128
LRU_C = 8.0
CONV_WIDTH = 4
SB_HEAD_DIM = 128
C_HEADS = 16
C_KV_HEADS = 2
C_GROUP = C_HEADS // C_KV_HEADS
C_HEAD_DIM = 64
WINDOW = 128
LN_EPS = 1e-5
STICK_DEAD = -100.0
LOG2E = math.log2(math.e)


def _sigmoid(x):
    return 0.5 * jnp.tanh(0.5 * x) + 0.5


def _silu(x):
    return x * _sigmoid(x)


def _vmem_limit(nbytes):
    return int(min(nbytes, VMEM_BYTES_V7X - 8 * 1024 * 1024))


def _inproj_kernel(x_ref, w_ref, *out_refs, widths, scales):
    xb = x_ref[...].astype(BF16)
    col = 0
    for o_ref, width, scale in zip(out_refs, widths, scales):
        r = jnp.dot(xb, w_ref[:, col:col + width], preferred_element_type=F32)
        if scale != 1.0:
            r = r * scale
        o_ref[...] = r.astype(o_ref.dtype)
        col += width


def _inproj(x2, w_bf16, widths, dtypes, scales, *, tm, name):
    t, d = x2.shape
    n = w_bf16.shape[1]
    assert sum(widths) == n and t % tm == 0
    out_bytes = sum(tm * wd * jnp.dtype(dt).itemsize for wd, dt in zip(widths, dtypes))
    vmem = d * n * 2 + 2 * tm * d * 4 + 2 * out_bytes + tm * d * 2 + 3 * tm * max(widths) * 4
    return pl.pallas_call(
        functools.partial(_inproj_kernel, widths=tuple(widths), scales=tuple(scales)),
        grid=(t // tm,),
        in_specs=[
            pl.BlockSpec((tm, d), lambda i: (i, 0)),
            pl.BlockSpec((d, n), lambda i: (0, 0), pipeline_mode=pl.Buffered(1)),
        ],
        out_specs=[pl.BlockSpec((tm, wd), lambda i: (i, 0)) for wd in widths],
        out_shape=[jax.ShapeDtypeStruct((t, wd), dt) for wd, dt in zip(widths, dtypes)],
        compiler_params=pltpu.CompilerParams(
            dimension_semantics=("arbitrary",), vmem_limit_bytes=_vmem_limit(vmem + (8 << 20))),
        name=name,
    )(x2, w_bf16)


def _outproj_ln_kernel(*refs, n_y, alpha, next_widths, next_scales):
    y_refs = refs[:n_y]
    w_ref, x_ref, g_ref, b_ref = refs[n_y:n_y + 4]
    rest = refs[n_y + 4:]
    wn_ref = rest[0] if next_widths else None
    o_ref = rest[1] if next_widths else rest[0]
    acc = alpha * x_ref[...]
    row = 0
    for y_ref in y_refs:
        kdim = y_ref.shape[1]
        acc = acc + jnp.dot(y_ref[...], w_ref[row:row + kdim, :], preferred_element_type=F32)
        row += kdim
    mu = jnp.mean(acc, axis=-1, keepdims=True)
    dev = acc - mu
    var = jnp.mean(dev * dev, axis=-1, keepdims=True)
    out = dev * lax.rsqrt(var + LN_EPS) * g_ref[...] + b_ref[...]
    o_ref[...] = out
    if next_widths:
        ob = out.astype(BF16)
        col = 0
        for n_ref, width, scale in zip(rest[2:], next_widths, next_scales):
            r = jnp.dot(ob, wn_ref[:, col:col + width], preferred_element_type=F32)
            if scale != 1.0:
                r = r * scale
            n_ref[...] = r.astype(n_ref.dtype)
            col += width


def _outproj_ln(ys, w_bf16, x2, ln_g, ln_b, alpha, *, tm, name, next_proj=None):
    t, d = x2.shape
    kdim = w_bf16.shape[0]
    assert sum(y.shape[1] for y in ys) == kdim and t % tm == 0
    resident = lambda shape: pl.BlockSpec(shape, lambda i: (0, 0), pipeline_mode=pl.Buffered(1))
    rows = lambda width: pl.BlockSpec((tm, width), lambda i: (i, 0))
    vmem = kdim * d * 2 + 2 * tm * kdim * 2 + 4 * tm * d * 4 + 4 * tm * d * 4
    in_specs = [rows(y.shape[1]) for y in ys] + [resident((kdim, d)), rows(d), resident((1, d)),
                                                 resident((1, d))]
    operands = [*ys, w_bf16, x2, ln_g.reshape(1, d), ln_b.reshape(1, d)]
    out_specs = [rows(d)]
    out_shape = [jax.ShapeDtypeStruct((t, d), F32)]
    widths, scales = (), ()
    if next_proj is not None:
        w_next, widths, dtypes, scales = next_proj
        assert w_next.shape == (d, sum(widths))
        in_specs.append(resident(w_next.shape))
        operands.append(w_next)
        out_specs += [rows(wd) for wd in widths]
        out_shape += [jax.ShapeDtypeStruct((t, wd), dt) for wd, dt in zip(widths, dtypes)]
        vmem += d * sum(widths) * 2 + tm * d * 2 + 3 * tm * max(widths) * 4
        vmem += 2 * sum(tm * wd * jnp.dtype(dt).itemsize for wd, dt in zip(widths, dtypes))
    outs = pl.pallas_call(
        functools.partial(_outproj_ln_kernel, n_y=len(ys), alpha=alpha,
                          next_widths=tuple(widths), next_scales=tuple(scales)),
        grid=(t // tm,),
        in_specs=in_specs,
        out_specs=out_specs,
        out_shape=out_shape,
        compiler_params=pltpu.CompilerParams(
            dimension_semantics=("arbitrary",), vmem_limit_bytes=_vmem_limit(vmem + (8 << 20))),
        name=name,
    )(*operands)
    return outs[0] if next_proj is None else (outs[0], outs[1:])


def _rglru_block(xc, ag, gates, ba, bx, lam, h_prev, tc):
    log_a_unit = LRU_C * (jnp.minimum(lam, 0.0) - jnp.log(1.0 + jnp.exp(-jnp.abs(lam))))
    row8 = lax.broadcasted_iota(jnp.int32, (SUBLANES, LRU_BLOCK), 0)
    h = h_prev
    ys = []
    for g in range(tc // SUBLANES):
        rows = slice(g * SUBLANES, (g + 1) * SUBLANES)
        r = _sigmoid(gates[rows, :LRU_BLOCK] + ba)
        i = _sigmoid(gates[rows, LRU_BLOCK:] + bx)
        a = jnp.exp(log_a_unit * r)
        y = 1.0 - a * a
        u = jnp.where(y > 0.0, y * lax.rsqrt(y), 0.0) * (i * xc[rows])
        for d in (1, 2, 4):
            a_sh = jnp.where(row8 >= d, pltpu.roll(a, d, axis=0), 1.0)
            u_sh = jnp.where(row8 >= d, pltpu.roll(u, d, axis=0), 0.0)
            u = u + a * u_sh
            a = a * a_sh
        hg = u + a * h
        h = hg[SUBLANES - 1:SUBLANES]
        ys.append(hg * _silu(ag[rows]))
    return jnp.concatenate(ys, axis=0), h


def _conv_block(x, halo, cw, cb, tc):
    row8 = lax.broadcasted_iota(jnp.int32, (SUBLANES, LRU_BLOCK), 0)
    prev = halo
    out = []
    for g in range(tc // SUBLANES):
        cur = x[g * SUBLANES:(g + 1) * SUBLANES]
        xc = cb + cw[CONV_WIDTH - 1:CONV_WIDTH] * cur
        for d in range(1, CONV_WIDTH):
            xd = jnp.where(row8 < d, pltpu.roll(prev, d, axis=0), pltpu.roll(cur, d, axis=0))
            xc = xc + cw[CONV_WIDTH - 1 - d:CONV_WIDTH - d] * xd
        out.append(xc)
        prev = cur
    return jnp.concatenate(out, axis=0)


def _inproj_rglru_kernel(x_ref, w_ref, cw_ref, cb_ref, wg_ref, ba_ref, bx_ref, lam_ref,
                         ya_ref, *rest, width, out_widths, out_scales):
    out_refs, (halo_ref, h_ref, ax_ref, ag_ref) = rest[:-4], rest[-4:]
    tc = x_ref.shape[0]
    s = pl.program_id(1)
    nblk = width // LRU_BLOCK
    chunk = 2 * LRU_BLOCK

    @pl.when(s == 0)
    def _():
        ax_ref[...] = jnp.zeros_like(ax_ref)
        ag_ref[...] = jnp.zeros_like(ag_ref)

    @pl.when(s <= 1)
    def _():
        halo_ref[...] = jnp.zeros_like(halo_ref)
        h_ref[...] = jnp.zeros_like(h_ref)

    xb = x_ref[...].astype(BF16)
    n_out = sum(out_widths) // chunk

    def project(col):
        r = jnp.dot(xb, w_ref[:, col:col + chunk], preferred_element_type=F32)
        if col < width:
            ax_ref[:, col:col + chunk] = r
        elif col < 2 * width:
            ag_ref[:, col - width:col - width + chunk] = r
        else:
            off, idx = col - 2 * width, 0
            while off >= out_widths[idx]:
                off -= out_widths[idx]
                idx += 1
            if out_scales[idx] != 1.0:
                r = r * out_scales[idx]
            out_refs[idx][:, off:off + chunk] = r.astype(out_refs[idx].dtype)

    out_cols = [2 * width + c * chunk for c in range(n_out)]
    plan = []
    for n in range(nblk):
        if n % 2 == 1:
            own = [(n // 2) * chunk, width + (n // 2) * chunk]
        else:
            own = []
        take = (2 * width // chunk + n_out) // nblk - len(own)
        plan.append(own + out_cols[:take])
        out_cols = out_cols[take:]
    assert not out_cols

    for n in range(nblk):
        blk = slice(n * LRU_BLOCK, (n + 1) * LRU_BLOCK)
        x = ax_ref[:, blk]
        ag = ag_ref[:, blk]
        xc = _conv_block(x, halo_ref[:, blk], cw_ref[:, blk], cb_ref[:, blk], tc)
        halo_ref[:, blk] = x[tc - SUBLANES:]
        gates = jnp.dot(xc.astype(BF16), wg_ref[n], preferred_element_type=F32)
        for col in plan[n]:
            project(col)
        y, h_last = _rglru_block(xc, ag, gates, ba_ref[:, blk], bx_ref[:, blk], lam_ref[:, blk],
                                 h_ref[:, blk], tc)
        h_ref[:, blk] = h_last
        ya_ref[:, blk] = y.astype(ya_ref.dtype)


def _inproj_rglru(x2, w_bf16, conv_w, conv_b, w_gates_bf16, b_a, b_x, lam, out_widths, out_scales,
                  *, batch, tc):
    t, d = x2.shape
    seq = t // batch
    width = conv_w.shape[1]
    nt = seq // tc
    n = w_bf16.shape[1]
    assert seq % tc == 0 and n == 2 * width + sum(out_widths)
    cur = lambda wd: pl.BlockSpec((tc, wd), lambda b, s: (b * nt + jnp.minimum(s, nt - 1), 0))
    lag = lambda wd: pl.BlockSpec((tc, wd), lambda b, s: (b * nt + jnp.maximum(s - 1, 0), 0))
    resident = lambda shape: pl.BlockSpec(shape, lambda b, s: (0,) * len(shape),
                                          pipeline_mode=pl.Buffered(1))
    vmem = d * n * 2 + 2 * tc * d * 4 + 2 * tc * (width + sum(out_widths)) * 2 + 24 * tc * width * 4
    outs = pl.pallas_call(
        functools.partial(_inproj_rglru_kernel, width=width, out_widths=tuple(out_widths),
                          out_scales=tuple(out_scales)),
        grid=(batch, nt + 1),
        in_specs=[cur(d), resident((d, n)), resident(conv_w.shape), resident((1, width)),
                  resident(w_gates_bf16.shape), resident((1, width)), resident((1, width)),
                  resident((1, width))],
        out_specs=[lag(width)] + [cur(ow) for ow in out_widths],
        out_shape=[jax.ShapeDtypeStruct((t, width), BF16)]
                  + [jax.ShapeDtypeStruct((t, ow), BF16) for ow in out_widths],
        scratch_shapes=[pltpu.VMEM((SUBLANES, width), F32), pltpu.VMEM((1, width), F32),
                        pltpu.VMEM((tc, width), F32), pltpu.VMEM((tc, width), F32)],
        compiler_params=pltpu.CompilerParams(
            dimension_semantics=("arbitrary", "arbitrary"),
            vmem_limit_bytes=_vmem_limit(vmem + (8 << 20))),
        name="inproj_rglru",
    )(x2, w_bf16, conv_w, conv_b.reshape(1, width), w_gates_bf16, b_a.reshape(1, width),
      b_x.reshape(1, width), lam.reshape(1, width))
    return outs[0], outs[1:]


def _stickbreak_kernel(q_ref, k_ref, v_ref, g_ref, o_ref, ntri_ref, acc_ref, *, tq, nh):
    dh = SB_HEAD_DIM
    i = pl.program_id(2)

    @pl.when((pl.program_id(0) == 0) & (pl.program_id(1) == 0) & (i == 0))
    def _():
        r = lax.broadcasted_iota(jnp.int32, (tq, tq), 0)
        c = lax.broadcasted_iota(jnp.int32, (tq, tq), 1)
        ntri_ref[...] = jnp.where(r > c, -1.0, 0.0).astype(BF16)

    ntri = ntri_ref[...]

    def sweep(blocks, carries, first):
        chains = [(b, h) for b in range(len(blocks)) for h in range(nh)]
        starts = [pl.multiple_of(j * tq, tq) for j, _ in blocks]
        carries = list(carries)
        zs = []
        for b, h in chains:
            kb = k_ref[pl.ds(starts[b], tq), h * dh:(h + 1) * dh]
            zs.append(lax.dot_general(q_ref[:, h * dh:(h + 1) * dh], kb, (((1,), (1,)), ((), ())),
                                      preferred_element_type=F32))
        logits = []
        for (b, h), z in zip(chains, zs):
            mask = blocks[b][1]
            sp = jnp.maximum(z, 0.0) + jnp.log(1.0 + jnp.exp(-jnp.abs(z)))
            if mask is not None:
                sp = jnp.where(mask, sp, 0.0)
            suffix = jnp.dot(sp.astype(BF16), ntri, preferred_element_type=F32)
            logits.append((z - sp) + suffix + carries[h])
            carries[h] = carries[h] - jnp.sum(sp, axis=-1, keepdims=True)
        outs = [None] * nh
        for (b, h), lg in zip(chains, logits):
            mask = blocks[b][1]
            w = jnp.exp(lg)
            if mask is not None:
                w = jnp.where(mask, w, 0.0)
            vb = v_ref[pl.ds(starts[b], tq), h * dh:(h + 1) * dh]
            pv = jnp.dot(w.astype(BF16), vb, preferred_element_type=F32)
            outs[h] = pv if outs[h] is None else outs[h] + pv
        for h in range(nh):
            if first:
                acc_ref[:, h * dh:(h + 1) * dh] = outs[h]
            else:
                acc_ref[:, h * dh:(h + 1) * dh] += outs[h]
        return tuple(carries)

    def alive(carries):
        top = carries[0]
        for other in carries[1:]:
            top = jnp.maximum(top, other)
        return (jnp.max(top) > STICK_DEAD).astype(jnp.int32)

    r = lax.broadcasted_iota(jnp.int32, (tq, tq), 0)
    c = lax.broadcasted_iota(jnp.int32, (tq, tq), 1)
    causal = c < r
    has_prev = (jnp.zeros((tq, tq), jnp.int32) + i) >= 1
    carries = sweep([(i, causal), (jnp.maximum(i - 1, 0), has_prev)],
                    tuple(jnp.zeros((tq, 1), F32) for _ in range(nh)), True)

    def more(state):
        return jnp.logical_and(state[0] >= 0, state[1] > 0)

    def earlier_block(state):
        carries = sweep([(state[0], None)], state[2:], False)
        return (state[0] - 1, alive(carries)) + carries

    lax.while_loop(more, earlier_block, (i - 2, alive(carries)) + carries)
    o_ref[...] = (acc_ref[...] * _silu(g_ref[...].astype(F32))).astype(o_ref.dtype)


def _stickbreak(q, k, v, g, *, batch, tq, nh):
    t, width = q.shape
    seq = t // batch
    gw = nh * SB_HEAD_DIM
    groups = width // gw
    nq = seq // tq
    assert seq % tq == 0 and width % gw == 0
    q_blk = lambda b, h, i: (b * nq + i, h)
    kv_blk = lambda b, h, i: (b, h)
    vmem = 2 * 2 * seq * gw * 2 + 2 * nh * tq * tq * 8 * 4
    return pl.pallas_call(
        functools.partial(_stickbreak_kernel, tq=tq, nh=nh),
        grid=(batch, groups, nq),
        in_specs=[
            pl.BlockSpec((tq, gw), q_blk),
            pl.BlockSpec((seq, gw), kv_blk),
            pl.BlockSpec((seq, gw), kv_blk),
            pl.BlockSpec((tq, gw), q_blk),
        ],
        out_specs=pl.BlockSpec((tq, gw), q_blk),
        out_shape=jax.ShapeDtypeStruct((t, width), BF16),
        scratch_shapes=[
            pltpu.VMEM((tq, tq), BF16),
            pltpu.VMEM((tq, gw), F32),
        ],
        compiler_params=pltpu.CompilerParams(
            dimension_semantics=("arbitrary", "arbitrary", "arbitrary"),
            vmem_limit_bytes=_vmem_limit(vmem + (8 << 20))),
        name="stickbreak",
    )(q, k, v, g)


def _alibi_slope(h):
    return 2.0 ** (-8.0 * (h + 1) / C_HEADS)


def _swa_kernel(sink_ref, q_ref, kp_ref, kc_ref, vp_ref, vc_ref, g_ref, o_ref, *, tq, blocks_per_seq):
    assert tq == WINDOW
    nk = WINDOW + tq
    pairs = C_GROUP // 2
    first = (pl.program_id(0) % blocks_per_seq) == 0

    r = lax.broadcasted_iota(jnp.int32, (tq, WINDOW), 0)
    c = lax.broadcasted_iota(jnp.int32, (tq, WINDOW), 1)
    from_prev = c > r
    dist = jnp.where(from_prev, r - c + WINDOW, r - c)
    no_prev = jnp.where(first, WINDOW, -1)
    mdist = jnp.where(from_prev & (c <= no_prev), jnp.inf, dist.astype(F32))

    lane = lax.broadcasted_iota(jnp.int32, (nk, LANES), 1)
    low = lane < C_HEAD_DIM

    def halves(prev_ref, cur_ref):
        full = jnp.concatenate([prev_ref[...], cur_ref[...]], axis=0).astype(F32)
        swapped = pltpu.roll(full, C_HEAD_DIM, axis=1)
        zero = jnp.zeros_like(full)
        even = [jnp.where(low, full, zero), jnp.where(low, swapped, zero)]
        odd = [jnp.where(low, zero, swapped), jnp.where(low, zero, full)]
        return [[e.astype(BF16) for e in even], [o.astype(BF16) for o in odd]]

    k_ext = halves(kp_ref, kc_ref)
    v_ext = halves(vp_ref, vc_ref)

    chains = [(kv, parity) for kv in range(C_KV_HEADS) for parity in range(2)]
    cols = [[(kv * pairs + p) * LANES for p in range(pairs)] for kv in range(C_KV_HEADS)]
    scores = []
    for kv, parity in chains:
        qs = jnp.concatenate([q_ref[:, c0:c0 + LANES] for c0 in cols[kv]], axis=0)
        scores.append(lax.dot_general(qs, k_ext[parity][kv], (((1,), (1,)), ((), ())),
                                      preferred_element_type=F32))
    outs = [None] * C_KV_HEADS
    for (kv, parity), s in zip(chains, scores):
        probs, invs = [], []
        for p in range(pairs):
            head = kv * C_GROUP + 2 * p + parity
            sink = sink_ref[head] * LOG2E
            sh = s[p * tq:(p + 1) * tq]
            sc = jnp.where(from_prev, sh[:, :WINDOW], sh[:, WINDOW:]) - (_alibi_slope(head) * LOG2E) * mdist
            m = jnp.maximum(jnp.max(sc, axis=-1, keepdims=True), sink)
            pe = jnp.exp2(sc - m)
            denom = jnp.sum(pe, axis=-1, keepdims=True) + jnp.exp2(sink - m)
            unfolded = jnp.concatenate([jnp.where(from_prev, pe, 0.0), jnp.where(from_prev, 0.0, pe)],
                                       axis=1)
            probs.append(unfolded.astype(BF16))
            invs.append(1.0 / denom)
        o_par = jnp.dot(jnp.concatenate(probs, axis=0), v_ext[parity][kv],
                        preferred_element_type=F32)
        o_par = o_par * jnp.concatenate(invs, axis=0)
        outs[kv] = o_par if outs[kv] is None else outs[kv] + o_par
    for kv in range(C_KV_HEADS):
        for p, c0 in enumerate(cols[kv]):
            gate = _silu(g_ref[:, c0:c0 + LANES].astype(F32))
            o_ref[:, c0:c0 + LANES] = (outs[kv][p * tq:(p + 1) * tq] * gate).astype(o_ref.dtype)


def _swa(q, k, v, g, sinks, *, batch, tq):
    t, width = q.shape
    seq = t // batch
    assert seq % tq == 0 and tq % WINDOW == 0 and k.shape[1] == LANES
    blocks_per_seq = seq // tq
    ratio = tq // WINDOW
    cur = lambda i: (i, 0)
    prev = lambda i: (jnp.maximum(i * ratio - 1, 0), 0)
    return pl.pallas_call(
        functools.partial(_swa_kernel, tq=tq, blocks_per_seq=blocks_per_seq),
        grid=(t // tq,),
        in_specs=[
            pl.BlockSpec(memory_space=pltpu.SMEM),
            pl.BlockSpec((tq, width), cur),
            pl.BlockSpec((WINDOW, LANES), prev),
            pl.BlockSpec((tq, LANES), cur),
            pl.BlockSpec((WINDOW, LANES), prev),
            pl.BlockSpec((tq, LANES), cur),
            pl.BlockSpec((tq, width), cur),
        ],
        out_specs=pl.BlockSpec((tq, width), cur),
        out_shape=jax.ShapeDtypeStruct((t, width), BF16),
        compiler_params=pltpu.CompilerParams(
            dimension_semantics=("arbitrary",), vmem_limit_bytes=_vmem_limit(32 << 20)),
        name="swa",
    )(sinks, q, k, k, v, v, g)


def _odd_inproj_spec(d, w_in):
    kv_width = C_KV_HEADS * C_HEAD_DIM
    return (w_in.astype(BF16), [d, kv_width, kv_width, d], [BF16] * 4,
            [LOG2E / math.sqrt(C_HEAD_DIM), 1.0, 1.0, 1.0])


def _even_layer(x2, batch, w_in, conv_w, conv_b, w_gate_a, b_gate_a, w_gate_x, b_gate_x, lam,
                w_out, ln_g, ln_b, alpha, next_proj):
    d = x2.shape[1]
    w_gates = jnp.concatenate([w_gate_a, w_gate_x], axis=-1).astype(BF16)
    y_a, (b_q, b_k, b_v, b_g) = _inproj_rglru(
        x2, w_in.astype(BF16), conv_w, conv_b, w_gates, b_gate_a, b_gate_x, lam,
        out_widths=[d] * 4, out_scales=[1.0 / math.sqrt(SB_HEAD_DIM), 1.0, 1.0, 1.0],
        batch=batch, tc=256)
    y_b = _stickbreak(b_q, b_k, b_v, b_g, batch=batch, tq=256, nh=4)
    return _outproj_ln([y_a, y_b], w_out.astype(BF16), x2, ln_g, ln_b, alpha, tm=1024,
                       name="outproj_ln_even", next_proj=next_proj)


def _odd_layer(x2, proj, batch, w_in, sinks, w_out, ln_g, ln_b, alpha):
    if proj is None:
        w_bf16, widths, dtypes, scales = _odd_inproj_spec(x2.shape[1], w_in)
        proj = _inproj(x2, w_bf16, widths=widths, dtypes=dtypes, scales=scales, tm=512,
                       name="inproj_odd")
    q, k, v, g = proj
    y = _swa(q, k, v, g, sinks, batch=batch, tq=WINDOW)
    return _outproj_ln([y], w_out.astype(BF16), x2, ln_g, ln_b, alpha, tm=1024, name="outproj_ln_odd")


def kernel(x, e_w_in, e_conv_w, e_conv_b, e_w_gate_a, e_b_gate_a, e_w_gate_x, e_b_gate_x, e_lru_lambda, e_w_out, e_ln_g, e_ln_b, o_w_in, o_sinks, o_w_out, o_ln_g, o_ln_b):
    batch, seq, d = x.shape
    depth = e_w_in.shape[0] + o_w_in.shape[0]
    alpha = float((2 * depth) ** 0.25)
    x2 = x.reshape(batch * seq, d)
    proj = None
    for layer in range(depth):
        li = layer // 2
        if layer % 2 == 0:
            next_proj = _odd_inproj_spec(d, o_w_in[li]) if layer + 1 < depth else None
            res = _even_layer(x2, batch, e_w_in[li], e_conv_w[li], e_conv_b[li], e_w_gate_a[li],
                              e_b_gate_a[li], e_w_gate_x[li], e_b_gate_x[li], e_lru_lambda[li],
                              e_w_out[li], e_ln_g[li], e_ln_b[li], alpha, next_proj)
            x2, proj = res if next_proj is not None else (res, None)
        else:
            x2 = _odd_layer(x2, proj, batch, o_w_in[li], o_sinks[li], o_w_out[li], o_ln_g[li],
                            o_ln_b[li], alpha)
            proj = None
    return x2.reshape(batch, seq, d)
```
